```python
import math
import jax
import jax.numpy as jnp
from jax import lax
import numpy as np

D_MODEL = 4096
BATCH = 2
SEQ = 8192
DEPTH = 2

N_EVEN = (DEPTH + 1) // 2
N_ODD = DEPTH // 2
HEAD_DIM = 128
MIX_HEADS = D_MODEL // HEAD_DIM
A_HEADS = MIX_HEADS // 2
A_LATENT = 512
IDX_HEADS = 16
IDX_DIM = 64
TOPK_MAX = 256
B_HEADS = MIX_HEADS - A_HEADS
B_KV_HEADS = 2
B_GROUP = B_HEADS // B_KV_HEADS
WINDOW = 128
C_HEADS = D_MODEL // (2 * HEAD_DIM)
C_VDIM = 2 * HEAD_DIM
D_FF = 7 * D_MODEL // 2
N_EXPERTS = 8
TOP_K = 2
D_FF_EXPERT = D_MODEL
NUM_BUCKETS = 32
MAX_DISTANCE = 128
BIAS_SLOTS = 2 * C_HEADS
Q_BLOCK = 128
RMS_EPS = 1e-6
SUBLN_EPS = 1e-5

AB_WIDTHS = (A_HEADS * HEAD_DIM, A_LATENT, IDX_HEADS * IDX_DIM, IDX_DIM, IDX_HEADS,
             B_HEADS * HEAD_DIM, B_KV_HEADS * HEAD_DIM, B_KV_HEADS * HEAD_DIM)
AB_SPLITS = tuple(int(v) for v in np.cumsum(AB_WIDTHS)[:-1])
AB_IN = sum(AB_WIDTHS)
C_IN = 4 * C_HEADS * HEAD_DIM + C_HEADS * C_VDIM

kernel_name = 'hybrid_dsa_swa_diffattn_moe_trunk'


def rmsnorm(x, g, eps=RMS_EPS):
    xf = x.astype(jnp.float32)
    y = xf * lax.rsqrt(jnp.mean(xf * xf, axis=-1, keepdims=True) + eps)
    return (y * g.astype(jnp.float32)).astype(x.dtype)


def t5_bucket(rel):
    n = jnp.maximum(rel, 0)
    max_exact = NUM_BUCKETS // 2
    nf = jnp.maximum(n, 1).astype(jnp.float32)
    large = max_exact + (jnp.log(nf / max_exact) / math.log(MAX_DISTANCE / max_exact)
                         * (NUM_BUCKETS - max_exact)).astype(jnp.int32)
    large = jnp.minimum(large, NUM_BUCKETS - 1)
    return jnp.where(n < max_exact, n, large)


def dsa_attention(q, c_kv, q_i, k_i, w_i, w_uk, w_uv, bias_a):
    bsz, s_len = q.shape[0], q.shape[1]
    topk = min(TOPK_MAX, s_len // 4)
    nb = s_len // Q_BLOCK
    s_pos = jnp.arange(s_len)

    def blocks(a):
        return a.reshape(bsz, nb, Q_BLOCK, *a.shape[2:]).swapaxes(0, 1)

    def one_block(args):
        qb, qib, wib, t0 = args
        t = t0 + jnp.arange(Q_BLOCK)
        dots = jnp.einsum('bthd,bsd->bths', qib, k_i) * (IDX_DIM ** -0.5)
        score = jnp.einsum('bths,bth->bts', jax.nn.relu(dots), wib).astype(jnp.float32)
        causal = s_pos[None, :] <= t[:, None]
        score = jnp.where(causal[None], score, -jnp.inf)
        _, idx = lax.top_k(score, topk)
        valid = idx <= t[None, :, None]
        c_sel = jax.vmap(lambda c, i: c[i])(c_kv, idx)
        q_lat = jnp.einsum('bthd,hdc->bthc', qb, w_uk)
        logits = jnp.einsum('bthc,btkc->bhtk', q_lat, c_sel).astype(jnp.float32) * (HEAD_DIM ** -0.5)
        bias = bias_a[t5_bucket(t[None, :, None] - idx)]
        logits = logits + jnp.moveaxis(bias, -1, 1).astype(jnp.float32)
        logits = jnp.where(valid[:, None], logits, -jnp.inf)
        p = jax.nn.softmax(logits, axis=-1).astype(qb.dtype)
        o_lat = jnp.einsum('bhtk,btkc->bthc', p, c_sel)
        return jnp.einsum('bthc,hcd->bthd', o_lat, w_uv)

    out = lax.map(one_block, (blocks(q), blocks(q_i), blocks(w_i), jnp.arange(nb) * Q_BLOCK))
    return out.swapaxes(0, 1).reshape(bsz, s_len, A_HEADS, HEAD_DIM)


def swa_sink_attention(q, k, v, bias_b, sinks):
    bsz, s_len = q.shape[0], q.shape[1]
    nb = s_len // WINDOW
    qb = q.reshape(bsz, nb, WINDOW, B_KV_HEADS, B_GROUP, HEAD_DIM)

    def band(a):
        a = a.reshape(bsz, nb, WINDOW, B_KV_HEADS, HEAD_DIM)
        prev = jnp.concatenate([jnp.zeros_like(a[:, :1]), a[:, :-1]], axis=1)
        return jnp.concatenate([prev, a], axis=2)

    kb, vb = band(k), band(v)
    i = jnp.arange(WINDOW)[:, None]
    j = jnp.arange(2 * WINDOW)[None, :]
    rel = i + WINDOW - j
    in_window = (rel >= 0) & (rel < WINDOW)
    s_global = jnp.arange(nb)[:, None, None] * WINDOW - WINDOW + j[None]
    mask = in_window[None] & (s_global >= 0)
    bias = jnp.moveaxis(bias_b[t5_bucket(rel)], -1, 0).reshape(B_KV_HEADS, B_GROUP, WINDOW, 2 * WINDOW)
    logits = jnp.einsum('bnqkgd,bnjkd->bnkgqj', qb, kb).astype(jnp.float32) * (HEAD_DIM ** -0.5)
    logits = logits + bias.astype(jnp.float32)
    logits = jnp.where(mask[None, :, None, None], logits, -jnp.inf)
    sink = jnp.broadcast_to(sinks.reshape(B_KV_HEADS, B_GROUP, 1, 1).astype(jnp.float32),
                            logits.shape[:-1] + (1,))
    p = jax.nn.softmax(jnp.concatenate([logits, sink], axis=-1), axis=-1)[..., :-1].astype(q.dtype)
    o = jnp.einsum('bnkgqj,bnjkd->bnqkgd', p, vb)
    return o.reshape(bsz, s_len, B_HEADS, HEAD_DIM)


def dsa_swa_mixer(h, w_in, kv_norm, w_uk, w_uv, sinks, w_out, rel_bias):
    bsz, s_len, _ = h.shape
    proj = h @ w_in
    q_a, c_kv, q_i, k_i, w_i, q_b, k_b, v_b = jnp.split(proj, AB_SPLITS, axis=-1)
    q_a = q_a.reshape(bsz, s_len, A_HEADS, HEAD_DIM)
    c_kv = rmsnorm(c_kv, kv_norm)
    q_i = q_i.reshape(bsz, s_len, IDX_HEADS, IDX_DIM)
    w_i = w_i * (IDX_HEADS ** -0.5)
    o_a = dsa_attention(q_a, c_kv, q_i, k_i, w_i, w_uk, w_uv, rel_bias[:, :A_HEADS])
    o_b = swa_sink_attention(q_b.reshape(bsz, s_len, B_HEADS, HEAD_DIM),
                             k_b.reshape(bsz, s_len, B_KV_HEADS, HEAD_DIM),
                             v_b.reshape(bsz, s_len, B_KV_HEADS, HEAD_DIM),
                             rel_bias[:, A_HEADS:], sinks)
    o = jnp.concatenate([o_a.reshape(bsz, s_len, -1), o_b.reshape(bsz, s_len, -1)], axis=-1)
    return o @ w_out


def diff_attention(h, w_in, lq1, lk1, lq2, lk2, subln, w_out, rel_bias, lambda_init):
    bsz, s_len, _ = h.shape
    proj = h @ w_in
    q, k, v = jnp.split(proj, [2 * C_HEADS * HEAD_DIM, 4 * C_HEADS * HEAD_DIM], axis=-1)
    q = q.reshape(bsz, s_len, 2, C_HEADS, HEAD_DIM)
    k = k.reshape(bsz, s_len, 2, C_HEADS, HEAD_DIM)
    v = v.reshape(bsz, s_len, C_HEADS, C_VDIM)
    lam = (jnp.exp(jnp.sum(lq1.astype(jnp.float32) * lk1.astype(jnp.float32)))
           - jnp.exp(jnp.sum(lq2.astype(jnp.float32) * lk2.astype(jnp.float32))) + lambda_init)
    nb = s_len // Q_BLOCK
    s_pos = jnp.arange(s_len)
    bias_tab = rel_bias.reshape(NUM_BUCKETS, 2, C_HEADS)
    q_blocks = q.reshape(bsz, nb, Q_BLOCK, 2, C_HEADS, HEAD_DIM).swapaxes(0, 1)

    def one_block(args):
        qb, t0 = args
        t = t0 + jnp.arange(Q_BLOCK)
        rel = t[:, None] - s_pos[None, :]
        bias = jnp.transpose(bias_tab[t5_bucket(rel)], (2, 3, 0, 1)).astype(jnp.float32)
        logits = jnp.einsum('btmhd,bsmhd->bmhts', qb, k).astype(jnp.float32) * (HEAD_DIM ** -0.5) + bias
        logits = jnp.where(rel >= 0, logits, -jnp.inf)
        p = jax.nn.softmax(logits, axis=-1)
        a = p[:, 0] - lam * p[:, 1]
        return jnp.einsum('bhts,bshe->bthe', a.astype(v.dtype), v)

    o = lax.map(one_block, (q_blocks, jnp.arange(nb) * Q_BLOCK))
    o = o.swapaxes(0, 1).reshape(bsz, s_len, C_HEADS, C_VDIM)
    o = rmsnorm(o, subln, SUBLN_EPS) * (1.0 - lambda_init)
    return o.reshape(bsz, s_len, -1) @ w_out


def swiglu(h, w_gate, w_up, w_down):
    return (jax.nn.silu(h @ w_gate) * (h @ w_up)) @ w_down


def moe_swiglu(h, w_router, w_gate, w_up, w_down):
    bsz, s_len, d = h.shape
    xt = h.reshape(-1, d)
    logits = (xt @ w_router).astype(jnp.float32)
    top_val, top_idx = lax.top_k(logits, TOP_K)
    gates = jax.nn.softmax(top_val, axis=-1)
    combine = jnp.sum(jax.nn.one_hot(top_idx, N_EXPERTS, dtype=jnp.float32) * gates[..., None], axis=1)
    out = jnp.zeros_like(xt)
    for e in range(N_EXPERTS):
        y = swiglu(xt, w_gate[e], w_up[e], w_down[e])
        out = out + combine[:, e:e + 1].astype(y.dtype) * y
    return out.reshape(bsz, s_len, d)


def setup_inputs(seed: int = 0) -> dict:
    key = jax.random.key(seed)
    ks = jax.random.split(key, 32)
    f32 = jnp.float32
    d = D_MODEL

    def nrm(k, shape, scale):
        return jax.random.normal(k, shape, f32) * scale

    return {
        'x': nrm(ks[0], (BATCH, SEQ, d), 1.0),
        'ln_mix': 1.0 + nrm(ks[1], (DEPTH, d), 0.02),
        'ln_ffn': 1.0 + nrm(ks[2], (DEPTH, d), 0.02),
        'rel_bias': nrm(ks[3], (NUM_BUCKETS, BIAS_SLOTS), 0.3),
        'w_in_ab': nrm(ks[4], (N_EVEN, d, AB_IN), d ** -0.5),
        'kv_norm': 1.0 + nrm(ks[5], (N_EVEN, A_LATENT), 0.02),
        'w_uk': nrm(ks[6], (N_EVEN, A_HEADS, HEAD_DIM, A_LATENT), A_LATENT ** -0.5),
        'w_uv': nrm(ks[7], (N_EVEN, A_HEADS, A_LATENT, HEAD_DIM), A_LATENT ** -0.5),
        'sinks': nrm(ks[8], (N_EVEN, B_HEADS), 1.0),
        'w_out_ab': nrm(ks[9], (N_EVEN, MIX_HEADS * HEAD_DIM, d), (MIX_HEADS * HEAD_DIM) ** -0.5),
        'w_gate_d': nrm(ks[10], (N_EVEN, d, D_FF), d ** -0.5),
        'w_up_d': nrm(ks[11], (N_EVEN, d, D_FF), d ** -0.5),
        'w_down_d': nrm(ks[12], (N_EVEN, D_FF, d), D_FF ** -0.5),
        'w_in_c': nrm(ks[13], (N_ODD, d, C_IN), d ** -0.5),
        'lambda_q1': nrm(ks[14], (N_ODD, HEAD_DIM), 0.1),
        'lambda_k1': nrm(ks[15], (N_ODD, HEAD_DIM), 0.1),
        'lambda_q2': nrm(ks[16], (N_ODD, HEAD_DIM), 0.1),
        'lambda_k2': nrm(ks[17], (N_ODD, HEAD_DIM), 0.1),
        'subln': 1.0 + nrm(ks[18], (N_ODD, C_VDIM), 0.02),
        'w_out_c': nrm(ks[19], (N_ODD, C_HEADS * C_VDIM, d), (C_HEADS * C_VDIM) ** -0.5),
        'w_router': nrm(ks[20], (N_ODD, d, N_EXPERTS), d ** -0.5),
        'w_gate_e': nrm(ks[21], (N_ODD, N_EXPERTS, d, D_FF_EXPERT), d ** -0.5),
        'w_up_e': nrm(ks[22], (N_ODD, N_EXPERTS, d, D_FF_EXPERT), d ** -0.5),
        'w_down_e': nrm(ks[23], (N_ODD, N_EXPERTS, D_FF_EXPERT, d), D_FF_EXPERT ** -0.5),
        'ln_final': 1.0 + nrm(ks[24], (d,), 0.02),
    }


def reference(x, ln_mix, ln_ffn, rel_bias, w_in_ab, kv_norm, w_uk, w_uv, sinks, w_out_ab,
              w_gate_d, w_up_d, w_down_d, w_in_c, lambda_q1, lambda_k1, lambda_q2, lambda_k2,
              subln, w_out_c, w_router, w_gate_e, w_up_e, w_down_e, ln_final):
    h = x
    for i in range(DEPTH):
        j = i // 2
        hn = rmsnorm(h, ln_mix[i])
        if i % 2 == 0:
            h = h + dsa_swa_mixer(hn, w_in_ab[j], kv_norm[j], w_uk[j], w_uv[j], sinks[j],
                                  w_out_ab[j], rel_bias)
            h = h + swiglu(rmsnorm(h, ln_ffn[i]), w_gate_d[j], w_up_d[j], w_down_d[j])
        else:
            lambda_init = 0.8 - 0.6 * math.exp(-0.3 * i)
            h = h + diff_attention(hn, w_in_c[j], lambda_q1[j], lambda_k1[j], lambda_q2[j],
                                   lambda_k2[j], subln[j], w_out_c[j], rel_bias, lambda_init)
            h = h + moe_swiglu(rmsnorm(h, ln_ffn[i]), w_router[j], w_gate_e[j], w_up_e[j], w_down_e[j])
    return rmsnorm(h, ln_final)
```

```python
import functools
import math

import numpy as np
import jax
import jax.numpy as jnp
from jax import lax
from jax.experimental import pallas as pl
from jax.experimental.pallas import tpu as pltpu

F32 = jnp.float32
BF16 = jnp.bfloat16
I32 = jnp.int32

HEAD_DIM = 128
A_LATENT = 512
IDX_HEADS = 16
IDX_DIM = 64
TOPK_MAX = 256
B_KV_HEADS = 2
WINDOW = 128
NUM_BUCKETS = 32
MAX_DISTANCE = 128
TOP_K = 2
RMS_EPS = 1e-6
SUBLN_EPS = 1e-5

LANES = 128
SUBLANES = 8
VMEM_LIMIT_BYTES = 56 * 1024 * 1024
INT_MIN = -(2 ** 31)
NEG_INF = float("-inf")

GATHER_STRIDE = TOPK_MAX + SUBLANES


def _bucket_thresholds():
    n = np.arange(0, 4 * MAX_DISTANCE, dtype=np.int64)
    max_exact = NUM_BUCKETS // 2
    nf = np.maximum(n, 1).astype(np.float32)
    large = max_exact + (
        np.log(nf / np.float32(max_exact)) / np.float32(math.log(MAX_DISTANCE / max_exact))
        * np.float32(NUM_BUCKETS - max_exact)
    ).astype(np.int32)
    large = np.minimum(large, NUM_BUCKETS - 1)
    bucket = np.where(n < max_exact, n, large)
    assert np.all(np.diff(bucket) >= 0) and bucket[-1] == NUM_BUCKETS - 1
    return tuple(int(np.argmax(bucket >= k)) for k in range(1, NUM_BUCKETS))


BUCKET_THRESHOLDS = _bucket_thresholds()
FAR_DISTANCE = BUCKET_THRESHOLDS[-1]


def _bias_from_distance(dist, table_entry):
    val = jnp.where(dist >= BUCKET_THRESHOLDS[0], table_entry(1), table_entry(0))
    for k in range(2, NUM_BUCKETS):
        val = jnp.where(dist >= BUCKET_THRESHOLDS[k - 1], table_entry(k), val)
    return val


def _pick_tile(dim, target, align=LANES):
    if dim <= target:
        return dim
    t = (target // align) * align
    while t >= align:
        if dim % t == 0:
            return t
        t -= align
    return dim


def _params(*semantics):
    return pltpu.CompilerParams(dimension_semantics=semantics, vmem_limit_bytes=VMEM_LIMIT_BYTES)


def _rmsnorm_kernel(x_ref, g_ref, o_ref, *, eps):
    x = x_ref[...].astype(F32)
    y = x * lax.rsqrt(jnp.mean(x * x, axis=-1, keepdims=True) + eps)
    o_ref[...] = (y * g_ref[...].astype(F32)).astype(o_ref.dtype)


def rmsnorm(x2d, g, eps, out_dtype):
    m, d = x2d.shape
    tm = _pick_tile(m, 256, SUBLANES)
    return pl.pallas_call(
        functools.partial(_rmsnorm_kernel, eps=eps),
        grid=(m // tm,),
        in_specs=[pl.BlockSpec((tm, d), lambda i: (i, 0)), pl.BlockSpec((1, d), lambda i: (0, 0))],
        out_specs=pl.BlockSpec((tm, d), lambda i: (i, 0)),
        out_shape=jax.ShapeDtypeStruct((m, d), out_dtype),
        compiler_params=_params("parallel"),
        name="rmsnorm",
    )(x2d, g.reshape(1, d))


def _matmul_kernel(*refs, nk, has_res, scale_col):
    a_ref, w_ref = refs[0], refs[1]
    pos = 2
    res_ref = scale_ref = None
    if has_res:
        res_ref = refs[pos]
        pos += 1
    if scale_col is not None:
        scale_ref = refs[pos]
        pos += 1
    o_ref, acc_ref = refs[pos], refs[pos + 1]
    k = pl.program_id(2)

    @pl.when(k == 0)
    def _():
        acc_ref[...] = jnp.zeros_like(acc_ref)

    acc_ref[...] += jnp.dot(a_ref[...], w_ref[...], preferred_element_type=F32)

    @pl.when(k == nk - 1)
    def _():
        y = acc_ref[...]
        if scale_ref is not None:
            y = y * scale_ref[:, scale_col:scale_col + 1]
        if res_ref is not None:
            y = res_ref[...].astype(F32) + y
        o_ref[...] = y.astype(o_ref.dtype)


def matmul(a, w, *, out_dtype, res=None, row_scale=None, scale_col=None, expert=None,
           tm=1024, tn=1024, tk=512):
    m, kdim = a.shape
    n = w.shape[-1]
    tm, tn, tk = _pick_tile(m, tm, SUBLANES), _pick_tile(n, tn), _pick_tile(kdim, tk)
    nk = kdim // tk
    if expert is None:
        w_spec = pl.BlockSpec((tk, tn), lambda i, j, k: (k, j))
    else:
        w_spec = pl.BlockSpec((None, tk, tn), lambda i, j, k: (expert, k, j))
    in_specs = [pl.BlockSpec((tm, tk), lambda i, j, k: (i, k)), w_spec]
    args = [a, w]
    if res is not None:
        in_specs.append(pl.BlockSpec((tm, tn), lambda i, j, k: (i, j)))
        args.append(res)
    if row_scale is not None:
        in_specs.append(pl.BlockSpec((tm, row_scale.shape[1]), lambda i, j, k: (i, 0)))
        args.append(row_scale)
    return pl.pallas_call(
        functools.partial(_matmul_kernel, nk=nk, has_res=res is not None,
                          scale_col=scale_col if row_scale is not None else None),
        grid=(m // tm, n // tn, nk),
        in_specs=in_specs,
        out_specs=pl.BlockSpec((tm, tn), lambda i, j, k: (i, j)),
        out_shape=jax.ShapeDtypeStruct((m, n), out_dtype),
        scratch_shapes=[pltpu.VMEM((tm, tn), F32)],
        compiler_params=_params("parallel", "parallel", "arbitrary"),
        name="matmul",
    )(*args)


def _swiglu_up_kernel(a_ref, wg_ref, wu_ref, o_ref, accg_ref, accu_ref, *, nk):
    k = pl.program_id(2)

    @pl.when(k == 0)
    def _():
        accg_ref[...] = jnp.zeros_like(accg_ref)
        accu_ref[...] = jnp.zeros_like(accu_ref)

    a = a_ref[...]
    accg_ref[...] += jnp.dot(a, wg_ref[...], preferred_element_type=F32)
    accu_ref[...] += jnp.dot(a, wu_ref[...], preferred_element_type=F32)

    @pl.when(k == nk - 1)
    def _():
        g = accg_ref[...]
        silu = g * (1.0 / (1.0 + jnp.exp(-g)))
        o_ref[...] = (silu * accu_ref[...]).astype(o_ref.dtype)


def swiglu_up(a, wg, wu, *, expert=None, tm=1024, tn=512, tk=512):
    m, kdim = a.shape
    f = wg.shape[-1]
    tm, tn, tk = _pick_tile(m, tm, SUBLANES), _pick_tile(f, tn), _pick_tile(kdim, tk)
    nk = kdim // tk
    if expert is None:
        w_spec = pl.BlockSpec((tk, tn), lambda i, j, k: (k, j))
    else:
        w_spec = pl.BlockSpec((None, tk, tn), lambda i, j, k: (expert, k, j))
    return pl.pallas_call(
        functools.partial(_swiglu_up_kernel, nk=nk),
        grid=(m // tm, f // tn, nk),
        in_specs=[pl.BlockSpec((tm, tk), lambda i, j, k: (i, k)), w_spec, w_spec],
        out_specs=pl.BlockSpec((tm, tn), lambda i, j, k: (i, j)),
        out_shape=jax.ShapeDtypeStruct((m, f), BF16),
        scratch_shapes=[pltpu.VMEM((tm, tn), F32), pltpu.VMEM((tm, tn), F32)],
        compiler_params=_params("parallel", "parallel", "arbitrary"),
        name="swiglu_up",
    )(a, wg, wu)


def _head_matmul_kernel(a_ref, w_ref, o_ref):
    o_ref[...] = jnp.dot(a_ref[...].astype(BF16), w_ref[...], preferred_element_type=F32).astype(o_ref.dtype)


def head_matmul(a, w3, *, out_dtype, tm=1024):
    m = a.shape[0]
    nheads, ka, nb = w3.shape
    tm = _pick_tile(m, tm, SUBLANES)
    return pl.pallas_call(
        _head_matmul_kernel,
        grid=(m // tm, nheads),
        in_specs=[pl.BlockSpec((tm, ka), lambda i, h: (i, h)),
                  pl.BlockSpec((None, ka, nb), lambda i, h: (h, 0, 0))],
        out_specs=pl.BlockSpec((tm, nb), lambda i, h: (i, h)),
        out_shape=jax.ShapeDtypeStruct((m, nheads * nb), out_dtype),
        compiler_params=_params("parallel", "parallel"),
        name="head_matmul",
    )(a, w3)


def _bits(x):
    return lax.bitcast_convert_type(x, I32)


def _latent_pack_kernel(c_ref, g_ref, o_ref, *, eps):
    x = c_ref[...]
    y = x * lax.rsqrt(jnp.mean(x * x, axis=-1, keepdims=True) + eps) * g_ref[...]
    yb = _bits(y.astype(BF16).astype(F32))
    for r in range(A_LATENT // (2 * LANES)):
        lo = lax.shift_right_logical(yb[:, (2 * r) * LANES:(2 * r + 1) * LANES], 16)
        hi = yb[:, (2 * r + 1) * LANES:(2 * r + 2) * LANES] & jnp.int32(-65536)
        o_ref[r] = lo | hi


def latent_pack(proj, col_block, g, eps):
    m = proj.shape[0]
    tm = _pick_tile(m, 512, SUBLANES)
    planes = A_LATENT // (2 * LANES)
    return pl.pallas_call(
        functools.partial(_latent_pack_kernel, eps=eps),
        grid=(m // tm,),
        in_specs=[pl.BlockSpec((tm, A_LATENT), lambda i: (i, col_block)),
                  pl.BlockSpec((1, A_LATENT), lambda i: (0, 0))],
        out_specs=pl.BlockSpec((planes, tm, LANES), lambda i: (0, i, 0)),
        out_shape=jax.ShapeDtypeStruct((planes, m, LANES), I32),
        compiler_params=_params("parallel"),
        name="latent_pack",
    )(proj, g.reshape(1, A_LATENT))


def _dsa_index_kernel(qi_ref, kw_ref, kwq_ref, idx_ref, qt_ref, key_ref, cs_ref, acc_ref, bnd_ref,
                      *, tq, topk, s_len):
    ch = tq
    i = pl.program_id(1)
    nchunks = i + 1

    q_t = qi_ref[0].T
    for h in range(IDX_HEADS):
        qt_ref[:, h * tq:(h + 1) * tq] = q_t[h * IDX_DIM:(h + 1) * IDX_DIM, :].astype(BF16)
    w_t = kwq_ref[0].T[IDX_DIM:IDX_DIM + IDX_HEADS, :] * (IDX_DIM ** -0.5 * IDX_HEADS ** -0.5)
    t_row = i * tq + lax.broadcasted_iota(I32, (1, tq), 1)
    s_iota = lax.broadcasted_iota(I32, (ch, tq), 0)

    def chunk_start(c):
        return pl.multiple_of(c * ch, ch)

    def score_chunk(c, carry):
        r0 = chunk_start(c)
        kc = kw_ref[0, pl.ds(r0, ch), :][:, :IDX_DIM].astype(BF16)
        dots = jnp.dot(kc, qt_ref[...], preferred_element_type=F32)
        sc = jnp.zeros((ch, tq), F32)
        for h in range(IDX_HEADS):
            sc = sc + jnp.maximum(dots[:, h * tq:(h + 1) * tq], 0.0) * w_t[h:h + 1, :]
        bits = _bits(sc)
        key = bits ^ ((bits >> 31) & jnp.int32(0x7FFFFFFF))
        key_ref[pl.ds(r0, ch), :] = jnp.where(r0 + s_iota <= t_row, key, jnp.int32(INT_MIN))
        return carry

    lax.fori_loop(0, nchunks, score_chunk, 0)

    def count(pred):
        def body(c, part):
            m = jnp.where(pred(key_ref[pl.ds(chunk_start(c), ch), :]), 1, 0).astype(I32)
            return part + m.reshape(ch // SUBLANES, SUBLANES, tq).sum(axis=0)
        part = lax.fori_loop(0, nchunks, body, jnp.zeros((SUBLANES, tq), I32))
        return part.sum(axis=0, keepdims=True)

    thr = jnp.where(count(lambda k: k >= 0) >= topk, 0, INT_MIN).astype(I32)

    def bit_step(b, thr):
        cand = thr | lax.shift_left(jnp.int32(1), 30 - b)
        return jnp.where(count(lambda k: k >= cand) >= topk, cand, thr)

    thr = lax.fori_loop(0, 31, bit_step, thr)
    need = (topk - count(lambda k: k > thr)).astype(F32)

    tri = (lax.broadcasted_iota(I32, (ch, ch), 0) >= lax.broadcasted_iota(I32, (ch, ch), 1)).astype(BF16)

    def select_chunk(c, carry):
        run, eq_run = carry
        r0 = chunk_start(c)
        k = key_ref[pl.ds(r0, ch), :]
        eq = jnp.where(k == thr, jnp.where(k > INT_MIN, 1.0, 0.0), 0.0)
        eq_cum = jnp.dot(tri, eq.astype(BF16), preferred_element_type=F32) + eq_run
        sel = jnp.where(k > thr, 1.0, jnp.where(eq_cum <= need, eq, 0.0))
        cs = jnp.dot(tri, sel.astype(BF16), preferred_element_type=F32) + run
        cs_ref[pl.ds(r0, ch), :] = cs
        new_run = cs[ch - 1:ch, :]
        bnd_ref[0, c] = jnp.min(run).astype(I32)
        bnd_ref[1, c] = jnp.max(new_run).astype(I32)
        return new_run, eq_cum[ch - 1:ch, :]

    zero_row = jnp.zeros((1, tq), F32)
    lax.fori_loop(0, nchunks, select_chunk, (zero_row, zero_row))

    acc_ref[...] = jnp.zeros_like(acc_ref)
    slot_iota = lax.broadcasted_iota(I32, (topk, tq), 0)
    sub_iota = lax.broadcasted_iota(I32, (SUBLANES, tq), 0)

    def compact_chunk(c, carry):
        r0 = chunk_start(c)
        jlo = bnd_ref[0, c]
        jhi = jnp.minimum(bnd_ref[1, c], topk)

        def slot_step(j, carry):
            m = jnp.where(cs_ref[pl.ds(r0, ch), :] <= j.astype(F32), 1.0, 0.0)
            cnt = m.reshape(ch // SUBLANES, SUBLANES, tq).sum(axis=0).sum(axis=0, keepdims=True)
            j8 = pl.multiple_of((j >> 3) << 3, SUBLANES)
            acc_ref[pl.ds(j8, SUBLANES), :] += jnp.where(sub_iota == (j & 7), cnt, 0.0)
            return carry

        lax.fori_loop(jlo, jhi, slot_step, 0)
        acc_ref[...] += jnp.where(slot_iota >= jhi, float(ch), 0.0)
        return carry

    lax.fori_loop(0, nchunks, compact_chunk, 0)
    idx_ref[0] = jnp.minimum(acc_ref[...], float(s_len - 1)).T.astype(I32)


def dsa_index(proj3, qi_block, kw_block, topk, tq=128):
    bsz, s_len, _ = proj3.shape
    tq = min(tq, s_len)
    nq = s_len // tq
    return pl.pallas_call(
        functools.partial(_dsa_index_kernel, tq=tq, topk=topk, s_len=s_len),
        grid=(bsz, nq),
        in_specs=[pl.BlockSpec((1, tq, IDX_HEADS * IDX_DIM), lambda b, i: (b, i, qi_block)),
                  pl.BlockSpec((1, s_len, LANES), lambda b, i: (b, 0, kw_block)),
                  pl.BlockSpec((1, tq, LANES), lambda b, i: (b, i, kw_block))],
        out_specs=pl.BlockSpec((1, tq, topk), lambda b, i: (b, i, 0)),
        out_shape=jax.ShapeDtypeStruct((bsz, s_len, topk), I32),
        scratch_shapes=[pltpu.VMEM((IDX_DIM, IDX_HEADS * tq), BF16),
                        pltpu.VMEM((s_len, tq), I32),
                        pltpu.VMEM((s_len, tq), F32),
                        pltpu.VMEM((topk, tq), F32),
                        pltpu.SMEM((2, nq), I32)],
        compiler_params=_params("parallel", "arbitrary"),
        name="dsa_index",
    )(proj3, proj3, proj3)


def _dsa_attn_kernel(idx_s_ref, idx_v_ref, ql_ref, ckv_ref, tab_ref, o_ref, tile_ref, *, tq, topk, nheads):
    i = pl.program_id(1)
    planes = A_LATENT // (2 * LANES)
    lane_iota = lax.broadcasted_iota(I32, (nheads, topk), 1)
    scale = HEAD_DIM ** -0.5

    def one_query(q, carry):
        for mi in range(topk):
            row = pl.multiple_of(idx_s_ref[0, q, mi] * planes, planes)
            tile_ref[pl.ds(mi, planes, stride=GATHER_STRIDE), :] = ckv_ref[0, pl.ds(row, planes), :]
        xs = []
        for r in range(planes):
            w = tile_ref[r * GATHER_STRIDE:r * GATHER_STRIDE + topk, :]
            xs.append(lax.bitcast_convert_type(lax.shift_left(w, 16), F32).astype(BF16))
            xs.append(lax.bitcast_convert_type(w & jnp.int32(-65536), F32).astype(BF16))
        ql = ql_ref[0, q]
        logits = jnp.zeros((nheads, topk), F32)
        for j, x in enumerate(xs):
            logits = logits + lax.dot_general(ql[:, j * LANES:(j + 1) * LANES], x, (((1,), (1,)), ((), ())),
                                              preferred_element_type=F32)
        t = i * tq + q
        dist = jnp.broadcast_to(t - idx_v_ref[0, q], (nheads, topk))
        bias = _bias_from_distance(dist, lambda k: tab_ref[:, k:k + 1])
        z = jnp.where(lane_iota < jnp.minimum(t + 1, topk), logits * scale + bias, NEG_INF)
        e = jnp.exp(z - jnp.max(z, axis=-1, keepdims=True))
        p = (e / jnp.sum(e, axis=-1, keepdims=True)).astype(BF16)
        for j, x in enumerate(xs):
            o_ref[0, q, :, j * LANES:(j + 1) * LANES] = jnp.dot(p, x, preferred_element_type=F32).astype(o_ref.dtype)
        return carry

    lax.fori_loop(0, tq, one_query, 0)


def dsa_attention(idx, q_lat, ckv_packed, bias_t, tq=32):
    bsz, s_len, topk = idx.shape
    nheads = q_lat.shape[2]
    tq = min(tq, s_len)
    planes = A_LATENT // (2 * LANES)
    return pl.pallas_call(
        functools.partial(_dsa_attn_kernel, tq=tq, topk=topk, nheads=nheads),
        grid=(bsz, s_len // tq),
        in_specs=[pl.BlockSpec((1, tq, topk), lambda b, i: (b, i, 0), memory_space=pltpu.SMEM),
                  pl.BlockSpec((1, tq, 1, topk), lambda b, i: (b, i, 0, 0)),
                  pl.BlockSpec((1, tq, nheads, A_LATENT), lambda b, i: (b, i, 0, 0)),
                  pl.BlockSpec((1, planes * s_len, LANES), lambda b, i: (b, 0, 0)),
                  pl.BlockSpec((nheads, LANES), lambda b, i: (0, 0))],
        out_specs=pl.BlockSpec((1, tq, nheads, A_LATENT), lambda b, i: (b, i, 0, 0)),
        out_shape=jax.ShapeDtypeStruct((bsz, s_len, nheads, A_LATENT), BF16),
        scratch_shapes=[pltpu.VMEM((planes * GATHER_STRIDE, LANES), I32)],
        compiler_params=_params("parallel", "arbitrary"),
        name="dsa_attention",
    )(idx, idx.reshape(bsz, s_len, 1, topk), q_lat, ckv_packed, bias_t)


def _swa_kernel(tab_ref, sink_ref, q_ref, kvp_ref, kvc_ref, o_ref, band_ref, *, nheads, slot0):
    w = WINDOW
    group = nheads // B_KV_HEADS
    b, n = pl.program_id(0), pl.program_id(1)
    rel = (lax.broadcasted_iota(I32, (w, 2 * w), 0) + w) - lax.broadcasted_iota(I32, (w, 2 * w), 1)

    @pl.when((b == 0) & (n == 0))
    def _():
        for hb in range(nheads):
            band_ref[hb] = _bias_from_distance(rel, lambda k: tab_ref[k, slot0 + hb])

    s_global = n * w - w + lax.broadcasted_iota(I32, (w, 2 * w), 1)
    mask = (rel >= 0) & (rel < w) & (s_global >= 0)
    scale = HEAD_DIM ** -0.5
    kv_width = B_KV_HEADS * HEAD_DIM
    for kh in range(B_KV_HEADS):
        ks = slice(kh * HEAD_DIM, (kh + 1) * HEAD_DIM)
        vs = slice(kv_width + kh * HEAD_DIM, kv_width + (kh + 1) * HEAD_DIM)
        kk = jnp.concatenate([kvp_ref[0][:, ks], kvc_ref[0][:, ks]], axis=0).astype(BF16)
        vv = jnp.concatenate([kvp_ref[0][:, vs], kvc_ref[0][:, vs]], axis=0).astype(BF16)
        for g in range(group):
            hb = kh * group + g
            cols = slice(hb * HEAD_DIM, (hb + 1) * HEAD_DIM)
            q = q_ref[0][:, cols].astype(BF16)
            z = lax.dot_general(q, kk, (((1,), (1,)), ((), ())), preferred_element_type=F32) * scale + band_ref[hb]
            z = jnp.where(mask, z, NEG_INF)
            sink = sink_ref[hb]
            m = jnp.maximum(jnp.max(z, axis=-1, keepdims=True), sink)
            e = jnp.exp(z - m)
            den = jnp.sum(e, axis=-1, keepdims=True) + jnp.exp(sink - m)
            o_ref[0, :, cols] = jnp.dot((e / den).astype(BF16), vv, preferred_element_type=F32).astype(o_ref.dtype)


def swa_attention(proj3, q_block, kv_block, rel_bias, sinks, slot0):
    bsz, s_len, _ = proj3.shape
    nheads = sinks.shape[0]
    qw = nheads * HEAD_DIM
    kvw = 2 * B_KV_HEADS * HEAD_DIM
    w = WINDOW
    smem = pl.BlockSpec(memory_space=pltpu.SMEM)
    return pl.pallas_call(
        functools.partial(_swa_kernel, nheads=nheads, slot0=slot0),
        grid=(bsz, s_len // w),
        in_specs=[smem, smem,
                  pl.BlockSpec((1, w, qw), lambda b, n: (b, n, q_block)),
                  pl.BlockSpec((1, w, kvw), lambda b, n: (b, jnp.maximum(n - 1, 0), kv_block)),
                  pl.BlockSpec((1, w, kvw), lambda b, n: (b, n, kv_block))],
        out_specs=pl.BlockSpec((1, w, qw), lambda b, n: (b, n, 0)),
        out_shape=jax.ShapeDtypeStruct((bsz, s_len, qw), BF16),
        scratch_shapes=[pltpu.VMEM((nheads, w, 2 * w), F32)],
        compiler_params=_params("arbitrary", "arbitrary"),
        name="swa_attention",
    )(rel_bias, sinks, proj3, proj3, proj3)


def _diff_attn_kernel(tab_ref, lam_ref, subln_ref, q1_ref, q2_ref, k1_ref, k2_ref, v_ref, o_ref,
                      band_ref, m_ref, l_ref, acc_ref, *, tq, nheads, lambda_init):
    h, qi = pl.program_id(1), pl.program_id(2)
    ii = lax.broadcasted_iota(I32, (tq, tq), 0)
    jj = lax.broadcasted_iota(I32, (tq, tq), 1)
    scale = HEAD_DIM ** -0.5
    last = NUM_BUCKETS - 1

    @pl.when(qi == 0)
    def _():
        for m in range(2):
            slot = m * nheads + h
            band_ref[m, 0] = _bias_from_distance(ii - jj, lambda k: tab_ref[k, slot])
            band_ref[m, 1] = _bias_from_distance(ii - jj + tq, lambda k: tab_ref[k, slot])

    m_ref[...] = jnp.full_like(m_ref, NEG_INF)
    l_ref[...] = jnp.zeros_like(l_ref)
    acc_ref[...] = jnp.zeros_like(acc_ref)
    qs = (q1_ref[0], q2_ref[0])
    k_refs = (k1_ref, k2_ref)

    def block(kj, bias_of_map, causal):
        r0 = pl.multiple_of(kj * tq, tq)
        v = v_ref[0, pl.ds(r0, tq), :]
        for m in range(2):
            k = k_refs[m][0, pl.ds(r0, tq), :]
            z = lax.dot_general(qs[m], k, (((1,), (1,)), ((), ())), preferred_element_type=F32) * scale
            z = z + bias_of_map(m)
            if causal:
                z = jnp.where(jj <= ii, z, NEG_INF)
            m_old = m_ref[m]
            m_new = jnp.maximum(m_old, jnp.max(z, axis=-1, keepdims=True))
            alpha = jnp.exp(m_old - m_new)
            e = jnp.exp(z - m_new)
            l_ref[m] = alpha * l_ref[m] + jnp.sum(e, axis=-1, keepdims=True)
            acc_ref[m] = alpha * acc_ref[m] + jnp.dot(e.astype(BF16), v, preferred_element_type=F32)
            m_ref[m] = m_new

    def far_block(kj, carry):
        block(kj, lambda m: tab_ref[last, m * nheads + h], causal=False)
        return carry

    lax.fori_loop(0, jnp.maximum(qi - 1, 0), far_block, 0)

    @pl.when(qi >= 1)
    def _():
        block(qi - 1, lambda m: band_ref[m, 1], causal=False)

    block(qi, lambda m: band_ref[m, 0], causal=True)

    lam = (jnp.exp(jnp.sum(lam_ref[0:1, :] * lam_ref[1:2, :], axis=-1, keepdims=True))
           - jnp.exp(jnp.sum(lam_ref[2:3, :] * lam_ref[3:4, :], axis=-1, keepdims=True)) + lambda_init)
    o = acc_ref[0] / l_ref[0] - lam * (acc_ref[1] / l_ref[1])
    y = o * lax.rsqrt(jnp.mean(o * o, axis=-1, keepdims=True) + SUBLN_EPS)
    o_ref[0] = ((y * subln_ref[...]) * (1.0 - lambda_init)).astype(o_ref.dtype)


def diff_attention(proj3, rel_bias, lam_vecs, subln, nheads, lambda_init, tq=256):
    bsz, s_len, _ = proj3.shape
    tq = min(tq, s_len)
    assert tq > FAR_DISTANCE
    vd = 2 * HEAD_DIM
    smem = pl.BlockSpec(memory_space=pltpu.SMEM)
    q_spec = [pl.BlockSpec((1, tq, HEAD_DIM), functools.partial(lambda m, b, h, i: (b, i, m * nheads + h), m))
              for m in range(2)]
    k_spec = [pl.BlockSpec((1, s_len, HEAD_DIM),
                           functools.partial(lambda m, b, h, i: (b, 0, (2 + m) * nheads + h), m))
              for m in range(2)]
    v_spec = pl.BlockSpec((1, s_len, vd), lambda b, h, i: (b, 0, 2 * nheads + h))
    return pl.pallas_call(
        functools.partial(_diff_attn_kernel, tq=tq, nheads=nheads, lambda_init=lambda_init),
        grid=(bsz, nheads, s_len // tq),
        in_specs=[smem, pl.BlockSpec((4, HEAD_DIM), lambda b, h, i: (0, 0)),
                  pl.BlockSpec((1, vd), lambda b, h, i: (0, 0)),
                  q_spec[0], q_spec[1], k_spec[0], k_spec[1], v_spec],
        out_specs=pl.BlockSpec((1, tq, vd), lambda b, h, i: (b, i, h)),
        out_shape=jax.ShapeDtypeStruct((bsz, s_len, nheads * vd), BF16),
        scratch_shapes=[pltpu.VMEM((2, 2, tq, tq), F32), pltpu.VMEM((2, tq, 1), F32),
                        pltpu.VMEM((2, tq, 1), F32), pltpu.VMEM((2, tq, vd), F32)],
        compiler_params=_params("arbitrary", "arbitrary", "arbitrary"),
        name="diff_attention",
    )(rel_bias, lam_vecs, subln.reshape(1, vd), proj3, proj3, proj3, proj3, proj3)


def _norm_router_kernel(x_ref, g_ref, wr_ref, xn_ref, comb_ref, *, eps, n_experts):
    x = x_ref[...]
    y = x * lax.rsqrt(jnp.mean(x * x, axis=-1, keepdims=True) + eps) * g_ref[...]
    xn_ref[...] = y.astype(xn_ref.dtype)
    logits = jnp.dot(y, wr_ref[...], preferred_element_type=F32, precision=lax.Precision.HIGHEST)
    lane = lax.broadcasted_iota(I32, logits.shape, 1)
    z = jnp.where(lane < n_experts, logits, NEG_INF)
    v1 = jnp.max(z, axis=-1, keepdims=True)
    i1 = jnp.min(jnp.where(z == v1, lane, LANES), axis=-1, keepdims=True)
    z2 = jnp.where(lane == i1, NEG_INF, z)
    v2 = jnp.max(z2, axis=-1, keepdims=True)
    i2 = jnp.min(jnp.where(z2 == v2, lane, LANES), axis=-1, keepdims=True)
    e2 = jnp.exp(v2 - v1)
    den = 1.0 + e2
    comb_ref[...] = jnp.where(lane == i1, 1.0 / den, 0.0) + jnp.where(lane == i2, e2 / den, 0.0)


def norm_router(x2d, g, w_router, eps):
    m, d = x2d.shape
    n_experts = w_router.shape[1]
    tm = _pick_tile(m, 256, SUBLANES)
    wr = jnp.pad(w_router.astype(F32), ((0, 0), (0, LANES - n_experts)))
    return pl.pallas_call(
        functools.partial(_norm_router_kernel, eps=eps, n_experts=n_experts),
        grid=(m // tm,),
        in_specs=[pl.BlockSpec((tm, d), lambda i: (i, 0)), pl.BlockSpec((1, d), lambda i: (0, 0)),
                  pl.BlockSpec((d, LANES), lambda i: (0, 0))],
        out_specs=[pl.BlockSpec((tm, d), lambda i: (i, 0)), pl.BlockSpec((tm, LANES), lambda i: (i, 0))],
        out_shape=[jax.ShapeDtypeStruct((m, d), BF16), jax.ShapeDtypeStruct((m, LANES), F32)],
        compiler_params=_params("parallel"),
        name="norm_router",
    )(x2d, g.reshape(1, d), wr)


def _dsa_swa_layer(h, hn, w_in, kv_norm, w_uk, w_uv, sinks, w_out, rel_bias, bsz, s_len):
    n_tok, d = h.shape
    a_heads = w_uk.shape[0]
    b_heads = sinks.shape[0]
    qa_w, qb_w = a_heads * HEAD_DIM, b_heads * HEAD_DIM
    kv_w = B_KV_HEADS * HEAD_DIM
    qi_w = IDX_HEADS * IDX_DIM
    widths = (qa_w, A_LATENT, qi_w, IDX_DIM, IDX_HEADS, qb_w, kv_w, kv_w)
    off = np.concatenate([[0], np.cumsum(widths)])
    seg = lambda s: w_in[:, int(off[s]):int(off[s + 1])]
    tail_pad = LANES - IDX_DIM - IDX_HEADS
    w_perm = jnp.concatenate([seg(0), seg(5), seg(6), seg(7), seg(1), seg(2), seg(3), seg(4),
                              jnp.zeros((d, tail_pad), w_in.dtype)], axis=1).astype(BF16)
    assert qa_w == qb_w and (qa_w + qb_w) % (2 * kv_w) == 0
    assert (qa_w + qb_w + 2 * kv_w) % A_LATENT == 0 and (qa_w + qb_w + 2 * kv_w + A_LATENT) % qi_w == 0
    kvb_block = (qa_w + qb_w) // (2 * kv_w)
    ckv_block = (qa_w + qb_w + 2 * kv_w) // A_LATENT
    qi_block = (qa_w + qb_w + 2 * kv_w + A_LATENT) // qi_w
    kw_block = (qa_w + qb_w + 2 * kv_w + A_LATENT + qi_w) // LANES

    proj = matmul(hn, w_perm, out_dtype=F32, tn=896)
    proj3 = proj.reshape(bsz, s_len, -1)
    topk = min(TOPK_MAX, s_len // 4)

    packed = latent_pack(proj, ckv_block, kv_norm, RMS_EPS)
    planes = packed.shape[0]
    ckv = jnp.transpose(packed, (1, 0, 2)).reshape(bsz, planes * s_len, LANES)
    idx = dsa_index(proj3, qi_block, kw_block, topk)
    q_lat = head_matmul(proj, w_uk.astype(BF16), out_dtype=BF16)
    bias_t = jnp.pad(rel_bias[:, :a_heads].T.astype(F32), ((0, 0), (0, LANES - NUM_BUCKETS)))
    o_lat = dsa_attention(idx, q_lat.reshape(bsz, s_len, a_heads, A_LATENT), ckv, bias_t)
    o_a = head_matmul(o_lat.reshape(n_tok, a_heads * A_LATENT), w_uv.astype(BF16), out_dtype=BF16)

    o_b = swa_attention(proj3, 1, kvb_block, rel_bias.astype(F32), sinks.astype(F32), a_heads)
    o = jnp.concatenate([o_a, o_b.reshape(n_tok, qb_w)], axis=1)
    return matmul(o, w_out.astype(BF16), out_dtype=F32, res=h)


def _diff_layer(h, hn, w_in, lam_vecs, subln, w_out, rel_bias, lambda_init, bsz, s_len):
    n_tok = h.shape[0]
    nheads = w_out.shape[0] // subln.shape[0]
    proj = matmul(hn, w_in.astype(BF16), out_dtype=BF16)
    o = diff_attention(proj.reshape(bsz, s_len, -1), rel_bias.astype(F32), lam_vecs.astype(F32),
                       subln.astype(F32), nheads, lambda_init)
    return matmul(o.reshape(n_tok, -1), w_out.astype(BF16), out_dtype=F32, res=h)


def _moe_layer(h, g, w_router, w_gate, w_up, w_down):
    xn, comb = norm_router(h, g, w_router, RMS_EPS)
    wg, wu, wd = w_gate.astype(BF16), w_up.astype(BF16), w_down.astype(BF16)
    for e in range(w_router.shape[1]):
        up = swiglu_up(xn, wg, wu, expert=e)
        h = matmul(up, wd, out_dtype=F32, res=h, row_scale=comb, scale_col=e, expert=e)
    return h


def kernel(x, ln_mix, ln_ffn, rel_bias, w_in_ab, kv_norm, w_uk, w_uv, sinks, w_out_ab, w_gate_d, w_up_d,
           w_down_d, w_in_c, lambda_q1, lambda_k1, lambda_q2, lambda_k2, subln, w_out_c, w_router,
           w_gate_e, w_up_e, w_down_e, ln_final):
    bsz, s_len, d = x.shape
    h = x.reshape(bsz * s_len, d)
    for i in range(ln_mix.shape[0]):
        j = i // 2
        hn = rmsnorm(h, ln_mix[i], RMS_EPS, BF16)
        if i % 2 == 0:
            h = _dsa_swa_layer(h, hn, w_in_ab[j], kv_norm[j], w_uk[j], w_uv[j], sinks[j], w_out_ab[j],
                               rel_bias, bsz, s_len)
            hf = rmsnorm(h, ln_ffn[i], RMS_EPS, BF16)
            up = swiglu_up(hf, w_gate_d[j].astype(BF16), w_up_d[j].astype(BF16))
            h = matmul(up, w_down_d[j].astype(BF16), out_dtype=F32, res=h)
        else:
            lambda_init = 0.8 - 0.6 * math.exp(-0.3 * i)
            lam_vecs = jnp.stack([lambda_q1[j], lambda_k1[j], lambda_q2[j], lambda_k2[j]])
            h = _diff_layer(h, hn, w_in_c[j], lam_vecs, subln[j], w_out_c[j], rel_bias, lambda_init,
                            bsz, s_len)
            h = _moe_layer(h, ln_ffn[i], w_router[j], w_gate_e[j], w_up_e[j], w_down_e[j])
    return rmsnorm(h, ln_final, RMS_EPS, x.dtype).reshape(bsz, s_len, d)
```

```python
import functools
import math

import numpy as np
import jax
import jax.numpy as jnp
from jax import lax
from jax.experimental import pallas as pl
from jax.experimental.pallas import tpu as pltpu

F32 = jnp.float32
BF16 = jnp.bfloat16
I32 = jnp.int32

HEAD_DIM = 128
A_LATENT = 512
IDX_HEADS = 16
IDX_DIM = 64
TOPK_MAX = 256
B_KV_HEADS = 2
WINDOW = 128
NUM_BUCKETS = 32
MAX_DISTANCE = 128
TOP_K = 2
RMS_EPS = 1e-6
SUBLN_EPS = 1e-5

LANES = 128
SUBLANES = 8
VMEM_LIMIT_BYTES = 56 * 1024 * 1024
INT_MIN = -(2 ** 31)
NEG_INF = float("-inf")

GATHER_STRIDE = TOPK_MAX + SUBLANES
LATENT_PLANES = A_LATENT // (2 * LANES)


def _bucket_thresholds():
    n = np.arange(0, 4 * MAX_DISTANCE, dtype=np.int64)
    max_exact = NUM_BUCKETS // 2
    nf = np.maximum(n, 1).astype(np.float32)
    large = max_exact + (
        np.log(nf / np.float32(max_exact)) / np.float32(math.log(MAX_DISTANCE / max_exact))
        * np.float32(NUM_BUCKETS - max_exact)
    ).astype(np.int32)
    large = np.minimum(large, NUM_BUCKETS - 1)
    bucket = np.where(n < max_exact, n, large)
    assert np.all(np.diff(bucket) >= 0) and bucket[-1] == NUM_BUCKETS - 1
    return tuple(int(np.argmax(bucket >= k)) for k in range(1, NUM_BUCKETS))


BUCKET_THRESHOLDS = _bucket_thresholds()
FAR_DISTANCE = BUCKET_THRESHOLDS[-1]


def _bias_from_distance(dist, table_entry):
    val = jnp.where(dist >= BUCKET_THRESHOLDS[0], table_entry(1), table_entry(0))
    for k in range(2, NUM_BUCKETS):
        val = jnp.where(dist >= BUCKET_THRESHOLDS[k - 1], table_entry(k), val)
    return val


def _pick_tile(dim, target, align=LANES):
    if dim <= target:
        return dim
    t = (target // align) * align
    while t >= align:
        if dim % t == 0:
            return t
        t -= align
    return dim


def _params(*semantics):
    return pltpu.CompilerParams(dimension_semantics=semantics, vmem_limit_bytes=VMEM_LIMIT_BYTES)


def _rmsnorm_kernel(x_ref, g_ref, o_ref, *, eps):
    x = x_ref[...].astype(F32)
    y = x * lax.rsqrt(jnp.mean(x * x, axis=-1, keepdims=True) + eps)
    o_ref[...] = (y * g_ref[...].astype(F32)).astype(o_ref.dtype)


def rmsnorm(x2d, g, eps, out_dtype):
    m, d = x2d.shape
    tm = _pick_tile(m, 256, SUBLANES)
    return pl.pallas_call(
        functools.partial(_rmsnorm_kernel, eps=eps),
        grid=(m // tm,),
        in_specs=[pl.BlockSpec((tm, d), lambda i: (i, 0)), pl.BlockSpec((1, d), lambda i: (0, 0))],
        out_specs=pl.BlockSpec((tm, d), lambda i: (i, 0)),
        out_shape=jax.ShapeDtypeStruct((m, d), out_dtype),
        compiler_params=_params("parallel"),
        name="rmsnorm",
    )(x2d, g.reshape(1, d))


def _matmul_kernel(*refs, nk, has_res, scale_col):
    a_ref, w_ref = refs[0], refs[1]
    pos = 2
    res_ref = scale_ref = None
    if has_res:
        res_ref = refs[pos]
        pos += 1
    if scale_col is not None:
        scale_ref = refs[pos]
        pos += 1
    o_ref = refs[pos]
    k = pl.program_id(2)

    def finish(y):
        if scale_ref is not None:
            y = y * scale_ref[:, scale_col:scale_col + 1]
        if res_ref is not None:
            y = res_ref[...].astype(F32) + y
        o_ref[...] = y.astype(o_ref.dtype)

    part = jnp.dot(a_ref[...], w_ref[...], preferred_element_type=F32)
    if nk == 1:
        finish(part)
        return
    acc_ref = refs[pos + 1]

    @pl.when(k == 0)
    def _():
        acc_ref[...] = part

    @pl.when(k > 0)
    def _():
        acc_ref[...] += part

    @pl.when(k == nk - 1)
    def _():
        finish(acc_ref[...])


def matmul(a, w, *, out_dtype, res=None, row_scale=None, scale_col=None, expert=None,
           tm=1024, tn=512, tk=4096):
    m, kdim = a.shape
    n = w.shape[-1]
    tm, tn, tk = _pick_tile(m, tm, SUBLANES), _pick_tile(n, tn), _pick_tile(kdim, tk)
    nk = kdim // tk
    if expert is None:
        w_spec = pl.BlockSpec((tk, tn), lambda i, j, k: (k, j))
    else:
        w_spec = pl.BlockSpec((None, tk, tn), lambda i, j, k: (expert, k, j))
    in_specs = [pl.BlockSpec((tm, tk), lambda i, j, k: (i, k)), w_spec]
    args = [a, w]
    if res is not None:
        in_specs.append(pl.BlockSpec((tm, tn), lambda i, j, k: (i, j)))
        args.append(res)
    if row_scale is not None:
        in_specs.append(pl.BlockSpec((tm, row_scale.shape[1]), lambda i, j, k: (i, 0)))
        args.append(row_scale)
    return pl.pallas_call(
        functools.partial(_matmul_kernel, nk=nk, has_res=res is not None,
                          scale_col=scale_col if row_scale is not None else None),
        grid=(m // tm, n // tn, nk),
        in_specs=in_specs,
        out_specs=pl.BlockSpec((tm, tn), lambda i, j, k: (i, j)),
        out_shape=jax.ShapeDtypeStruct((m, n), out_dtype),
        scratch_shapes=[pltpu.VMEM((tm, tn), F32)] if nk > 1 else [],
        compiler_params=_params("parallel", "parallel", "arbitrary"),
        name="matmul",
    )(*args)


def _swiglu_up_kernel(a_ref, wg_ref, wu_ref, o_ref, *acc_refs, nk):
    k = pl.program_id(2)

    def finish(g, u):
        silu = g * (1.0 / (1.0 + jnp.exp(-g)))
        o_ref[...] = (silu * u).astype(o_ref.dtype)

    a = a_ref[...]
    gate = jnp.dot(a, wg_ref[...], preferred_element_type=F32)
    up = jnp.dot(a, wu_ref[...], preferred_element_type=F32)
    if nk == 1:
        finish(gate, up)
        return
    accg_ref, accu_ref = acc_refs

    @pl.when(k == 0)
    def _():
        accg_ref[...] = gate
        accu_ref[...] = up

    @pl.when(k > 0)
    def _():
        accg_ref[...] += gate
        accu_ref[...] += up

    @pl.when(k == nk - 1)
    def _():
        finish(accg_ref[...], accu_ref[...])


def swiglu_up(a, wg, wu, *, expert=None, tm=1024, tn=512, tk=4096):
    m, kdim = a.shape
    f = wg.shape[-1]
    tm, tn, tk = _pick_tile(m, tm, SUBLANES), _pick_tile(f, tn), _pick_tile(kdim, tk)
    nk = kdim // tk
    if expert is None:
        w_spec = pl.BlockSpec((tk, tn), lambda i, j, k: (k, j))
    else:
        w_spec = pl.BlockSpec((None, tk, tn), lambda i, j, k: (expert, k, j))
    return pl.pallas_call(
        functools.partial(_swiglu_up_kernel, nk=nk),
        grid=(m // tm, f // tn, nk),
        in_specs=[pl.BlockSpec((tm, tk), lambda i, j, k: (i, k)), w_spec, w_spec],
        out_specs=pl.BlockSpec((tm, tn), lambda i, j, k: (i, j)),
        out_shape=jax.ShapeDtypeStruct((m, f), BF16),
        scratch_shapes=[pltpu.VMEM((tm, tn), F32), pltpu.VMEM((tm, tn), F32)] if nk > 1 else [],
        compiler_params=_params("parallel", "parallel", "arbitrary"),
        name="swiglu_up",
    )(a, wg, wu)


def _head_matmul_kernel(a_ref, w_ref, o_ref):
    o_ref[...] = jnp.dot(a_ref[...].astype(BF16), w_ref[...], preferred_element_type=F32).astype(o_ref.dtype)


def head_matmul(a, w3, *, out_dtype, tm=1024):
    m = a.shape[0]
    nheads, ka, nb = w3.shape
    tm = _pick_tile(m, tm, SUBLANES)
    return pl.pallas_call(
        _head_matmul_kernel,
        grid=(m // tm, nheads),
        in_specs=[pl.BlockSpec((tm, ka), lambda i, h: (i, h)),
                  pl.BlockSpec((None, ka, nb), lambda i, h: (h, 0, 0))],
        out_specs=pl.BlockSpec((tm, nb), lambda i, h: (i, h)),
        out_shape=jax.ShapeDtypeStruct((m, nheads * nb), out_dtype),
        compiler_params=_params("parallel", "parallel"),
        name="head_matmul",
    )(a, w3)


def _bits(x):
    return lax.bitcast_convert_type(x, I32)


def _latent_pack_kernel(c_ref, g_ref, o_ref, *, eps):
    x = c_ref[...]
    y = x * lax.rsqrt(jnp.mean(x * x, axis=-1, keepdims=True) + eps) * g_ref[...]
    yb = _bits(y.astype(BF16).astype(F32))
    for r in range(A_LATENT // (2 * LANES)):
        lo = lax.shift_right_logical(yb[:, (2 * r) * LANES:(2 * r + 1) * LANES], 16)
        hi = yb[:, (2 * r + 1) * LANES:(2 * r + 2) * LANES] & jnp.int32(-65536)
        o_ref[r] = lo | hi


def latent_pack(proj, col_block, g, eps):
    m = proj.shape[0]
    tm = _pick_tile(m, 512, SUBLANES)
    planes = A_LATENT // (2 * LANES)
    return pl.pallas_call(
        functools.partial(_latent_pack_kernel, eps=eps),
        grid=(m // tm,),
        in_specs=[pl.BlockSpec((tm, A_LATENT), lambda i: (i, col_block)),
                  pl.BlockSpec((1, A_LATENT), lambda i: (0, 0))],
        out_specs=pl.BlockSpec((planes, tm, LANES), lambda i: (0, i, 0)),
        out_shape=jax.ShapeDtypeStruct((planes, m, LANES), I32),
        compiler_params=_params("parallel"),
        name="latent_pack",
    )(proj, g.reshape(1, A_LATENT))


def _dsa_index_kernel(qi_ref, kw_ref, kwq_ref, idx_ref, qt_ref, key_ref, cs_ref, acc_ref, bnd_ref,
                      *, tq, topk, s_len, row_scale):
    ch = tq
    i = pl.program_id(1)
    nchunks = i + 1

    q_t = qi_ref[0].T
    for h in range(IDX_HEADS):
        qt_ref[:, h * tq:(h + 1) * tq] = q_t[h * IDX_DIM:(h + 1) * IDX_DIM, :].astype(BF16)
    w_t = kwq_ref[0].T[IDX_DIM:IDX_DIM + IDX_HEADS, :] * (IDX_DIM ** -0.5 * IDX_HEADS ** -0.5)
    t_row = i * tq + lax.broadcasted_iota(I32, (1, tq), 1)
    s_iota = lax.broadcasted_iota(I32, (ch, tq), 0)

    def chunk_start(c):
        return pl.multiple_of(c * ch, ch)

    def score_chunk(c, carry):
        r0 = chunk_start(c)
        kc = kw_ref[0, pl.ds(r0, ch), :][:, :IDX_DIM].astype(BF16)
        dots = jnp.dot(kc, qt_ref[...], preferred_element_type=F32)
        sc = jnp.zeros((ch, tq), F32)
        for h in range(IDX_HEADS):
            sc = sc + jnp.maximum(dots[:, h * tq:(h + 1) * tq], 0.0) * w_t[h:h + 1, :]
        bits = _bits(sc)
        key = bits ^ ((bits >> 31) & jnp.int32(0x7FFFFFFF))
        key_ref[pl.ds(r0, ch), :] = jnp.where(r0 + s_iota <= t_row, key, jnp.int32(INT_MIN))
        return carry

    lax.fori_loop(0, nchunks, score_chunk, 0)

    def count(pred):
        def body(c, part):
            m = jnp.where(pred(key_ref[pl.ds(chunk_start(c), ch), :]), 1, 0).astype(I32)
            return part + m.reshape(ch // SUBLANES, SUBLANES, tq).sum(axis=0)
        part = lax.fori_loop(0, nchunks, body, jnp.zeros((SUBLANES, tq), I32))
        return part.sum(axis=0, keepdims=True)

    thr = jnp.where(count(lambda k: k >= 0) >= topk, 0, INT_MIN).astype(I32)

    def bit_step(b, thr):
        cand = thr | lax.shift_left(jnp.int32(1), 30 - b)
        return jnp.where(count(lambda k: k >= cand) >= topk, cand, thr)

    thr = lax.fori_loop(0, 31, bit_step, thr)
    need = (topk - count(lambda k: k > thr)).astype(F32)

    tri = (lax.broadcasted_iota(I32, (ch, ch), 0) >= lax.broadcasted_iota(I32, (ch, ch), 1)).astype(BF16)

    def select_chunk(c, carry):
        run, eq_run = carry
        r0 = chunk_start(c)
        k = key_ref[pl.ds(r0, ch), :]
        eq = jnp.where(k == thr, jnp.where(k > INT_MIN, 1.0, 0.0), 0.0)
        eq_cum = jnp.dot(tri, eq.astype(BF16), preferred_element_type=F32) + eq_run
        sel = jnp.where(k > thr, 1.0, jnp.where(eq_cum <= need, eq, 0.0))
        cs = jnp.dot(tri, sel.astype(BF16), preferred_element_type=F32) + run
        cs_ref[pl.ds(r0, ch), :] = cs
        new_run = cs[ch - 1:ch, :]
        bnd_ref[0, c] = jnp.min(run).astype(I32)
        bnd_ref[1, c] = jnp.max(new_run).astype(I32)
        return new_run, eq_cum[ch - 1:ch, :]

    zero_row = jnp.zeros((1, tq), F32)
    lax.fori_loop(0, nchunks, select_chunk, (zero_row, zero_row))

    acc_ref[...] = jnp.zeros_like(acc_ref)
    slot_iota = lax.broadcasted_iota(I32, (topk, tq), 0)
    sub_iota = lax.broadcasted_iota(I32, (SUBLANES, tq), 0)

    def compact_chunk(c, carry):
        r0 = chunk_start(c)
        jlo = bnd_ref[0, c]
        jhi = jnp.minimum(bnd_ref[1, c], topk)

        def slot_step(j, carry):
            m = jnp.where(cs_ref[pl.ds(r0, ch), :] <= j.astype(F32), 1.0, 0.0)
            cnt = m.reshape(ch // SUBLANES, SUBLANES, tq).sum(axis=0).sum(axis=0, keepdims=True)
            j8 = pl.multiple_of((j >> 3) << 3, SUBLANES)
            acc_ref[pl.ds(j8, SUBLANES), :] += jnp.where(sub_iota == (j & 7), cnt, 0.0)
            return carry

        lax.fori_loop(jlo, jhi, slot_step, 0)
        acc_ref[...] += jnp.where(slot_iota >= jhi, float(ch), 0.0)
        return carry

    lax.fori_loop(0, nchunks, compact_chunk, 0)
    idx_ref[0] = (jnp.minimum(acc_ref[...], float(s_len - 1)) * float(row_scale)).T.astype(I32)


def dsa_index(proj3, qi_block, kw_block, topk, row_scale=1, tq=128):
    bsz, s_len, _ = proj3.shape
    tq = min(tq, s_len)
    nq = s_len // tq
    return pl.pallas_call(
        functools.partial(_dsa_index_kernel, tq=tq, topk=topk, s_len=s_len, row_scale=row_scale),
        grid=(bsz, nq),
        in_specs=[pl.BlockSpec((1, tq, IDX_HEADS * IDX_DIM), lambda b, i: (b, i, qi_block)),
                  pl.BlockSpec((1, s_len, LANES), lambda b, i: (b, 0, kw_block)),
                  pl.BlockSpec((1, tq, LANES), lambda b, i: (b, i, kw_block))],
        out_specs=pl.BlockSpec((1, tq, topk), lambda b, i: (b, i, 0)),
        out_shape=jax.ShapeDtypeStruct((bsz, s_len, topk), I32),
        scratch_shapes=[pltpu.VMEM((IDX_DIM, IDX_HEADS * tq), BF16),
                        pltpu.VMEM((s_len, tq), I32),
                        pltpu.VMEM((s_len, tq), F32),
                        pltpu.VMEM((topk, tq), F32),
                        pltpu.SMEM((2, nq), I32)],
        compiler_params=_params("parallel", "arbitrary"),
        name="dsa_index",
    )(proj3, proj3, proj3)


def _dsa_attn_kernel(idx_s_ref, idx_v_ref, ql_ref, ckv_ref, tab_ref, o_ref, *tile_refs, tq, topk, nheads):
    i = pl.program_id(1)
    planes = LATENT_PLANES
    plane_shift = planes.bit_length() - 1
    assert planes == 1 << plane_shift
    lane_iota = lax.broadcasted_iota(I32, (nheads, topk), 1)
    scale = HEAD_DIM ** -0.5

    def gather(q, tile_ref, lo, hi):
        for mi in range(lo, hi):
            row = pl.multiple_of(idx_s_ref[0, q, mi], planes)
            tile_ref[pl.ds(mi, planes, stride=GATHER_STRIDE), :] = ckv_ref[0, pl.ds(row, planes), :]

    def attend(queries, tiles):
        nq = len(queries)
        xs = [[] for _ in queries]
        for r in range(planes):
            for n in range(nq):
                w = tiles[n][r * GATHER_STRIDE:r * GATHER_STRIDE + topk, :]
                xs[n].append(lax.bitcast_convert_type(lax.shift_left(w, 16), F32).astype(BF16))
                xs[n].append(lax.bitcast_convert_type(w & jnp.int32(-65536), F32).astype(BF16))
            yield
        logits = []
        for n, q in enumerate(queries):
            ql = ql_ref[0, q]
            acc = jnp.zeros((nheads, topk), F32)
            for j, x in enumerate(xs[n]):
                acc = acc + lax.dot_general(ql[:, j * LANES:(j + 1) * LANES], x, (((1,), (1,)), ((), ())),
                                            preferred_element_type=F32)
            logits.append(acc)
        yield
        bias = []
        for q in queries:
            dist = jnp.broadcast_to(i * tq + q - (idx_v_ref[0, q] >> plane_shift), (nheads, topk))
            bias.append(_bias_from_distance(dist, lambda k: tab_ref[:, k:k + 1]))
        yield
        es, inv = [], []
        for n, q in enumerate(queries):
            z = jnp.where(lane_iota < jnp.minimum(i * tq + q + 1, topk), logits[n] * scale + bias[n], NEG_INF)
            e = jnp.exp(z - jnp.max(z, axis=-1, keepdims=True))
            inv.append(1.0 / jnp.sum(e, axis=-1, keepdims=True))
            es.append(e.astype(BF16))
        yield
        for j in range(2 * planes):
            for n, q in enumerate(queries):
                o = jnp.dot(es[n], xs[n][j], preferred_element_type=F32) * inv[n]
                o_ref[0, q, :, j * LANES:(j + 1) * LANES] = o.astype(o_ref.dtype)
            if j % 2 == 1:
                yield

    n_stages = 2 * planes + 3
    rows_per_stage = topk // n_stages

    def attend_while_gathering(queries, tiles, next_queries, next_tiles):
        stages = attend(queries, tiles)
        for s in range(n_stages):
            next(stages)
            hi = topk if s == n_stages - 1 else (s + 1) * rows_per_stage
            for q, tile_ref in zip(next_queries, next_tiles):
                gather(q, tile_ref, s * rows_per_stage, hi)

    half = len(tile_refs) // 2
    first, second = tile_refs[:half], tile_refs[half:]
    for n in range(half):
        gather(n, first[n], 0, topk)

    def query_group(qg, carry):
        q0 = 2 * half * qg
        qa = [q0 + n for n in range(half)]
        qb = [q0 + half + n for n in range(half)]
        qc = [jnp.minimum(q0 + 2 * half + n, tq - 1) for n in range(half)]
        attend_while_gathering(qa, first, qb, second)
        attend_while_gathering(qb, second, qc, first)
        return carry

    lax.fori_loop(0, tq // (2 * half), query_group, 0)


def dsa_attention(idx, q_lat, ckv_packed, bias_t, tq=32):
    bsz, s_len, topk = idx.shape
    nheads = q_lat.shape[2]
    tq = min(tq, s_len)
    planes = LATENT_PLANES
    return pl.pallas_call(
        functools.partial(_dsa_attn_kernel, tq=tq, topk=topk, nheads=nheads),
        grid=(bsz, s_len // tq),
        in_specs=[pl.BlockSpec((1, tq, topk), lambda b, i: (b, i, 0), memory_space=pltpu.SMEM),
                  pl.BlockSpec((1, tq, 1, topk), lambda b, i: (b, i, 0, 0)),
                  pl.BlockSpec((1, tq, nheads, A_LATENT), lambda b, i: (b, i, 0, 0)),
                  pl.BlockSpec((1, planes * s_len, LANES), lambda b, i: (b, 0, 0)),
                  pl.BlockSpec((nheads, LANES), lambda b, i: (0, 0))],
        out_specs=pl.BlockSpec((1, tq, nheads, A_LATENT), lambda b, i: (b, i, 0, 0)),
        out_shape=jax.ShapeDtypeStruct((bsz, s_len, nheads, A_LATENT), BF16),
        scratch_shapes=[pltpu.VMEM((planes * GATHER_STRIDE, LANES), I32)] * 4,
        compiler_params=_params("parallel", "arbitrary"),
        name="dsa_attention",
    )(idx, idx.reshape(bsz, s_len, 1, topk), q_lat, ckv_packed, bias_t)


def _swa_kernel(tab_ref, sink_ref, q_ref, kvp_ref, kvc_ref, o_ref, band_ref, *, nheads, slot0):
    w = WINDOW
    group = nheads // B_KV_HEADS
    b, n = pl.program_id(0), pl.program_id(1)
    rel = (lax.broadcasted_iota(I32, (w, 2 * w), 0) + w) - lax.broadcasted_iota(I32, (w, 2 * w), 1)

    @pl.when((b == 0) & (n == 0))
    def _():
        for hb in range(nheads):
            band_ref[hb] = _bias_from_distance(rel, lambda k: tab_ref[k, slot0 + hb])

    s_global = n * w - w + lax.broadcasted_iota(I32, (w, 2 * w), 1)
    mask = (rel >= 0) & (rel < w) & (s_global >= 0)
    scale = HEAD_DIM ** -0.5
    kv_width = B_KV_HEADS * HEAD_DIM
    for kh in range(B_KV_HEADS):
        ks = slice(kh * HEAD_DIM, (kh + 1) * HEAD_DIM)
        vs = slice(kv_width + kh * HEAD_DIM, kv_width + (kh + 1) * HEAD_DIM)
        kk = jnp.concatenate([kvp_ref[0][:, ks], kvc_ref[0][:, ks]], axis=0).astype(BF16)
        vv = jnp.concatenate([kvp_ref[0][:, vs], kvc_ref[0][:, vs]], axis=0).astype(BF16)
        for g in range(group):
            hb = kh * group + g
            cols = slice(hb * HEAD_DIM, (hb + 1) * HEAD_DIM)
            q = q_ref[0][:, cols].astype(BF16)
            z = lax.dot_general(q, kk, (((1,), (1,)), ((), ())), preferred_element_type=F32) * scale + band_ref[hb]
            z = jnp.where(mask, z, NEG_INF)
            sink = sink_ref[hb]
            m = jnp.maximum(jnp.max(z, axis=-1, keepdims=True), sink)
            e = jnp.exp(z - m)
            den = jnp.sum(e, axis=-1, keepdims=True) + jnp.exp(sink - m)
            o_ref[0, :, cols] = jnp.dot((e / den).astype(BF16), vv, preferred_element_type=F32).astype(o_ref.dtype)


def swa_attention(proj3, q_block, kv_block, rel_bias, sinks, slot0):
    bsz, s_len, _ = proj3.shape
    nheads = sinks.shape[0]
    qw = nheads * HEAD_DIM
    kvw = 2 * B_KV_HEADS * HEAD_DIM
    w = WINDOW
    smem = pl.BlockSpec(memory_space=pltpu.SMEM)
    return pl.pallas_call(
        functools.partial(_swa_kernel, nheads=nheads, slot0=slot0),
        grid=(bsz, s_len // w),
        in_specs=[smem, smem,
                  pl.BlockSpec((1, w, qw), lambda b, n: (b, n, q_block)),
                  pl.BlockSpec((1, w, kvw), lambda b, n: (b, jnp.maximum(n - 1, 0), kv_block)),
                  pl.BlockSpec((1, w, kvw), lambda b, n: (b, n, kv_block))],
        out_specs=pl.BlockSpec((1, w, qw), lambda b, n: (b, n, 0)),
        out_shape=jax.ShapeDtypeStruct((bsz, s_len, qw), BF16),
        scratch_shapes=[pltpu.VMEM((nheads, w, 2 * w), F32)],
        compiler_params=_params("arbitrary", "arbitrary"),
        name="swa_attention",
    )(rel_bias, sinks, proj3, proj3, proj3)


def _diff_attn_kernel(tab_ref, lam_ref, subln_ref, q1_ref, q2_ref, k1_ref, k2_ref, v_ref, o_ref,
                      band_ref, m_ref, l_ref, acc_ref, *, tq, nheads, lambda_init):
    h, qi = pl.program_id(1), pl.program_id(2)
    ii = lax.broadcasted_iota(I32, (tq, tq), 0)
    jj = lax.broadcasted_iota(I32, (tq, tq), 1)
    log2e = 1.4426950408889634
    qk_scale = HEAD_DIM ** -0.5 * log2e
    last = NUM_BUCKETS - 1
    ntile = tq // LANES
    vd = acc_ref.shape[-1]

    @pl.when(qi == 0)
    def _():
        for m in range(2):
            slot = m * nheads + h
            far = tab_ref[last, slot]
            band_ref[m, 0] = (_bias_from_distance(ii - jj, lambda k: tab_ref[k, slot]) - far) * log2e
            band_ref[m, 1] = (_bias_from_distance(ii - jj + tq, lambda k: tab_ref[k, slot]) - far) * log2e

    m_ref[...] = jnp.full_like(m_ref, NEG_INF)
    l_ref[...] = jnp.zeros_like(l_ref)
    acc_ref[...] = jnp.zeros_like(acc_ref)
    qs = (q1_ref[0], q2_ref[0])
    k_refs = (k1_ref, k2_ref)

    def block(kj, band_idx, causal):
        r0 = pl.multiple_of(kj * tq, tq)
        v = v_ref[0, pl.ds(r0, tq), :]
        for m in range(2):
            k = k_refs[m][0, pl.ds(r0, tq), :]
            z = lax.dot_general(qs[m], k, (((1,), (1,)), ((), ())), preferred_element_type=F32) * qk_scale
            if band_idx is not None:
                z = z + band_ref[m, band_idx]
            if causal:
                z = jnp.where(jj <= ii, z, NEG_INF)
            zt = [z[:, j * LANES:(j + 1) * LANES] for j in range(ntile)]
            m_old = m_ref[m]
            m_new = jnp.maximum(m_old, jnp.max(functools.reduce(jnp.maximum, zt), axis=-1, keepdims=True))
            alpha = jnp.exp2(m_old - m_new)
            es = [jnp.exp2(t - m_new) for t in zt]
            l_ref[m] = alpha * l_ref[m] + functools.reduce(jnp.add, es)
            pv = jnp.dot(jnp.concatenate([e.astype(BF16) for e in es], axis=1), v, preferred_element_type=F32)
            for c in range(vd // LANES):
                cols = slice(c * LANES, (c + 1) * LANES)
                acc_ref[m, :, cols] = alpha * acc_ref[m, :, cols] + pv[:, cols]
            m_ref[m] = m_new

    def far_block(kj, carry):
        block(kj, None, causal=False)
        return carry

    lax.fori_loop(0, jnp.maximum(qi - 1, 0), far_block, 0)

    @pl.when(qi >= 1)
    def _():
        block(qi - 1, 1, causal=False)

    block(qi, 0, causal=True)

    lam = (jnp.exp(jnp.sum(lam_ref[0:1, :] * lam_ref[1:2, :], axis=-1, keepdims=True))
           - jnp.exp(jnp.sum(lam_ref[2:3, :] * lam_ref[3:4, :], axis=-1, keepdims=True)) + lambda_init)
    outs = [acc_ref[m] / jnp.sum(l_ref[m], axis=-1, keepdims=True) for m in range(2)]
    o = outs[0] - lam * outs[1]
    y = o * lax.rsqrt(jnp.mean(o * o, axis=-1, keepdims=True) + SUBLN_EPS)
    o_ref[0] = ((y * subln_ref[...]) * (1.0 - lambda_init)).astype(o_ref.dtype)


def diff_attention(proj3, rel_bias, lam_vecs, subln, nheads, lambda_init, tq=512):
    bsz, s_len, _ = proj3.shape
    tq = min(tq, s_len)
    assert tq > FAR_DISTANCE
    vd = 2 * HEAD_DIM
    smem = pl.BlockSpec(memory_space=pltpu.SMEM)
    q_spec = [pl.BlockSpec((1, tq, HEAD_DIM), functools.partial(lambda m, b, h, i: (b, i, m * nheads + h), m))
              for m in range(2)]
    k_spec = [pl.BlockSpec((1, s_len, HEAD_DIM),
                           functools.partial(lambda m, b, h, i: (b, 0, (2 + m) * nheads + h), m))
              for m in range(2)]
    v_spec = pl.BlockSpec((1, s_len, vd), lambda b, h, i: (b, 0, 2 * nheads + h))
    return pl.pallas_call(
        functools.partial(_diff_attn_kernel, tq=tq, nheads=nheads, lambda_init=lambda_init),
        grid=(bsz, nheads, s_len // tq),
        in_specs=[smem, pl.BlockSpec((4, HEAD_DIM), lambda b, h, i: (0, 0)),
                  pl.BlockSpec((1, vd), lambda b, h, i: (0, 0)),
                  q_spec[0], q_spec[1], k_spec[0], k_spec[1], v_spec],
        out_specs=pl.BlockSpec((1, tq, vd), lambda b, h, i: (b, i, h)),
        out_shape=jax.ShapeDtypeStruct((bsz, s_len, nheads * vd), BF16),
        scratch_shapes=[pltpu.VMEM((2, 2, tq, tq), F32), pltpu.VMEM((2, tq, LANES), F32),
                        pltpu.VMEM((2, tq, LANES), F32), pltpu.VMEM((2, tq, vd), F32)],
        compiler_params=_params("arbitrary", "arbitrary", "arbitrary"),
        name="diff_attention",
    )(rel_bias, lam_vecs, subln.reshape(1, vd), proj3, proj3, proj3, proj3, proj3)


def _norm_router_kernel(x_ref, g_ref, wr_ref, xn_ref, comb_ref, sel_ref, *, eps, n_experts):
    x = x_ref[...]
    y = x * lax.rsqrt(jnp.mean(x * x, axis=-1, keepdims=True) + eps) * g_ref[...]
    xn_ref[...] = y
    logits = jnp.dot(y, wr_ref[...], preferred_element_type=F32, precision=lax.Precision.HIGHEST)
    lane = lax.broadcasted_iota(I32, logits.shape, 1)
    z = jnp.where(lane < n_experts, logits, NEG_INF)
    v1 = jnp.max(z, axis=-1, keepdims=True)
    i1 = jnp.min(jnp.where(z == v1, lane, LANES), axis=-1, keepdims=True)
    z2 = jnp.where(lane == i1, NEG_INF, z)
    v2 = jnp.max(z2, axis=-1, keepdims=True)
    i2 = jnp.min(jnp.where(z2 == v2, lane, LANES), axis=-1, keepdims=True)
    e2 = jnp.exp(v2 - v1)
    den = 1.0 + e2
    comb_ref[...] = jnp.where(lane == i1, 1.0 / den, 0.0) + jnp.where(lane == i2, e2 / den, 0.0)
    sel_ref[...] = jnp.where(lane == i1, 1.0, 0.0) + jnp.where(lane == i2, 1.0, 0.0)


def norm_router(x2d, g, w_router, eps):
    m, d = x2d.shape
    n_experts = w_router.shape[1]
    tm = _pick_tile(m, 256, SUBLANES)
    wr = jnp.pad(w_router.astype(F32), ((0, 0), (0, LANES - n_experts)))
    lane_spec = pl.BlockSpec((tm, LANES), lambda i: (i, 0))
    return pl.pallas_call(
        functools.partial(_norm_router_kernel, eps=eps, n_experts=n_experts),
        grid=(m // tm,),
        in_specs=[pl.BlockSpec((tm, d), lambda i: (i, 0)), pl.BlockSpec((1, d), lambda i: (0, 0)),
                  pl.BlockSpec((d, LANES), lambda i: (0, 0))],
        out_specs=[pl.BlockSpec((tm, d), lambda i: (i, 0)), lane_spec, lane_spec],
        out_shape=[jax.ShapeDtypeStruct((m, d), F32), jax.ShapeDtypeStruct((m, LANES), F32),
                   jax.ShapeDtypeStruct((m, LANES), F32)],
        compiler_params=_params("parallel"),
        name="norm_router",
    )(x2d, g.reshape(1, d), wr)


MOE_ROW_TILE = 512
PLAN_ROWS = 256


def _moe_plan_kernel(sel_ref, comb_ref, dest_ref, gate_ref, tile_ref, *, n_experts, n_tok):
    rb = PLAN_ROWS
    nblk = n_tok // rb
    shift = MOE_ROW_TILE.bit_length() - 1
    lane = lax.broadcasted_iota(I32, (1, LANES), 1)

    def count_block(b, part):
        m = sel_ref[pl.ds(pl.multiple_of(b * rb, rb), rb), :]
        return part + m.reshape(rb // SUBLANES, SUBLANES, LANES).sum(axis=0)

    cnt = lax.fori_loop(0, nblk, count_block, jnp.zeros((SUBLANES, LANES), F32)).sum(axis=0, keepdims=True)
    padded = (((cnt.astype(I32) + (MOE_ROW_TILE - 1)) >> shift) << shift)
    start = jnp.zeros((1, LANES), I32)
    tile_expert = jnp.zeros((1, LANES), I32)
    for e in range(n_experts):
        begin = jnp.sum(jnp.where(lane < e, padded, 0), axis=-1, keepdims=True)
        end = jnp.sum(jnp.where(lane <= e, padded, 0), axis=-1, keepdims=True)
        start = start + jnp.where(lane == e, begin, 0)
        tile_expert = tile_expert + jnp.where(lane * MOE_ROW_TILE >= end, 1, 0)
    total = jnp.sum(jnp.where(lane < n_experts, padded, 0), axis=-1, keepdims=True)
    tile_ref[...] = jnp.zeros_like(tile_ref)
    tile_ref[0:1, :] = jnp.minimum(tile_expert, n_experts - 1)
    tile_ref[1:2, :] = jnp.where(lane * MOE_ROW_TILE < total, 1, 0)

    start_f = start.astype(F32)
    strict_lower = (lax.broadcasted_iota(I32, (rb, rb), 0) > lax.broadcasted_iota(I32, (rb, rb), 1)).astype(BF16)
    lane_b = lax.broadcasted_iota(I32, (rb, LANES), 1)
    big = float(2 ** 30)

    def rank_block(b, run):
        r0 = pl.multiple_of(b * rb, rb)
        m = sel_ref[pl.ds(r0, rb), :]
        before = jnp.dot(strict_lower, m.astype(BF16), preferred_element_type=F32) + run
        row = start_f + before
        lo = jnp.min(jnp.where(m > 0.0, row, big), axis=-1, keepdims=True)
        hi = jnp.max(jnp.where(m > 0.0, row, -1.0), axis=-1, keepdims=True)
        comb = comb_ref[pl.ds(r0, rb), :]
        g_lo = jnp.sum(jnp.where((m > 0.0) & (row == lo), comb, 0.0), axis=-1, keepdims=True)
        g_hi = jnp.sum(jnp.where((m > 0.0) & (row == hi), comb, 0.0), axis=-1, keepdims=True)
        gate_ref[pl.ds(r0, rb), :] = jnp.where(lane_b == 0, g_lo, jnp.where(lane_b == 1, g_hi, 0.0))
        rows_t = jnp.where(lane_b == 0, lo, jnp.where(lane_b == 1, hi, 0.0)).T
        dest_ref[:, pl.ds(r0, rb)] = rows_t[0:SUBLANES, :].astype(I32)
        return before[rb - 1:rb, :] + m[rb - 1:rb, :]

    lax.fori_loop(0, nblk, rank_block, jnp.zeros((1, LANES), F32))


def moe_plan(sel, comb, n_experts):
    n_tok = sel.shape[0]
    assert n_tok % PLAN_ROWS == 0
    return pl.pallas_call(
        functools.partial(_moe_plan_kernel, n_experts=n_experts, n_tok=n_tok),
        out_shape=[jax.ShapeDtypeStruct((SUBLANES, n_tok), I32), jax.ShapeDtypeStruct((n_tok, LANES), F32),
                   jax.ShapeDtypeStruct((SUBLANES, LANES), I32)],
        compiler_params=pltpu.CompilerParams(vmem_limit_bytes=VMEM_LIMIT_BYTES),
        name="moe_plan",
    )(sel, comb)


def _row_copy(src_ref, src_row, dst_ref, dst_row, sem):
    return pltpu.make_async_copy(src_ref.at[pl.ds(src_row, 1), :], dst_ref.at[pl.ds(dst_row, 1), :], sem)


def _moe_scatter_kernel(dest_ref, x_ref, init_ref, xs_ref, sem, *, tb):
    del init_ref
    base = pl.program_id(0) * tb

    def start(n, carry):
        for k in range(TOP_K):
            _row_copy(x_ref, base + n, xs_ref, dest_ref[k, n], sem).start()
        return carry

    def wait(n, carry):
        for k in range(TOP_K):
            _row_copy(x_ref, base + n, xs_ref, dest_ref[k, n], sem).wait()
        return carry

    lax.fori_loop(0, tb, start, 0)
    lax.fori_loop(0, tb, wait, 0)


def moe_scatter(dest, xn, rows):
    n_tok, d = xn.shape
    tb = _pick_tile(n_tok, 512)
    any_spec = pl.BlockSpec(memory_space=pl.ANY)
    return pl.pallas_call(
        functools.partial(_moe_scatter_kernel, tb=tb),
        grid=(n_tok // tb,),
        in_specs=[pl.BlockSpec((SUBLANES, tb), lambda i: (0, i), memory_space=pltpu.SMEM), any_spec, any_spec],
        out_specs=any_spec,
        out_shape=jax.ShapeDtypeStruct((rows, d), xn.dtype),
        scratch_shapes=[pltpu.SemaphoreType.DMA(())],
        input_output_aliases={2: 0},
        compiler_params=_params("arbitrary"),
        name="moe_scatter",
    )(dest, xn, jnp.zeros((rows, d), xn.dtype))


def _moe_up_kernel(te_ref, tv_ref, a_ref, wg_ref, wu_ref, o_ref, abf_ref):
    i, j = pl.program_id(0), pl.program_id(1)

    @pl.when(j == 0)
    def _():
        abf_ref[...] = a_ref[...].astype(BF16)

    @pl.when(tv_ref[i] == 0)
    def _():
        o_ref[...] = jnp.zeros_like(o_ref)

    @pl.when(tv_ref[i] != 0)
    def _():
        a = abf_ref[...]
        g = jnp.dot(a, wg_ref[...], preferred_element_type=F32)
        u = jnp.dot(a, wu_ref[...], preferred_element_type=F32)
        o_ref[...] = ((g * (1.0 / (1.0 + jnp.exp(-g)))) * u).astype(o_ref.dtype)


def moe_up(tile_expert, tile_valid, xs, wg, wu, tn=512):
    rows, kdim = xs.shape
    f = wg.shape[-1]
    tm, tn = MOE_ROW_TILE, _pick_tile(f, tn)
    w_spec = pl.BlockSpec((None, kdim, tn), lambda i, j, te, tv: (te[i], 0, j))
    return pl.pallas_call(
        _moe_up_kernel,
        grid_spec=pltpu.PrefetchScalarGridSpec(
            num_scalar_prefetch=2,
            grid=(rows // tm, f // tn),
            in_specs=[pl.BlockSpec((tm, kdim), lambda i, j, te, tv: (i, 0)), w_spec, w_spec],
            out_specs=pl.BlockSpec((tm, tn), lambda i, j, te, tv: (i, j)),
            scratch_shapes=[pltpu.VMEM((tm, kdim), BF16)]),
        out_shape=jax.ShapeDtypeStruct((rows, f), BF16),
        compiler_params=_params("arbitrary", "arbitrary"),
        name="moe_up",
    )(tile_expert, tile_valid, xs, wg, wu)


def _moe_down_kernel(te_ref, tv_ref, a_ref, w_ref, o_ref):
    i = pl.program_id(0)

    @pl.when(tv_ref[i] == 0)
    def _():
        o_ref[...] = jnp.zeros_like(o_ref)

    @pl.when(tv_ref[i] != 0)
    def _():
        o_ref[...] = jnp.dot(a_ref[...], w_ref[...], preferred_element_type=F32)


def moe_down(tile_expert, tile_valid, up, wd, tn=512):
    rows, f = up.shape
    d = wd.shape[-1]
    tm, tn = MOE_ROW_TILE, _pick_tile(d, tn)
    return pl.pallas_call(
        _moe_down_kernel,
        grid_spec=pltpu.PrefetchScalarGridSpec(
            num_scalar_prefetch=2,
            grid=(rows // tm, d // tn),
            in_specs=[pl.BlockSpec((tm, f), lambda i, j, te, tv: (i, 0)),
                      pl.BlockSpec((None, f, tn), lambda i, j, te, tv: (te[i], 0, j))],
            out_specs=pl.BlockSpec((tm, tn), lambda i, j, te, tv: (i, j))),
        out_shape=jax.ShapeDtypeStruct((rows, d), F32),
        compiler_params=_params("parallel", "parallel"),
        name="moe_down",
    )(tile_expert, tile_valid, up, wd)


def _moe_combine_kernel(dest_ref, gate_ref, h_ref, y_ref, o_ref, ylo_ref, yhi_ref, sem, *, tb):
    bufs = (ylo_ref, yhi_ref)

    def start(n, carry):
        for k in range(TOP_K):
            _row_copy(y_ref, dest_ref[k, n], bufs[k], n, sem).start()
        return carry

    def wait(n, carry):
        for k in range(TOP_K):
            _row_copy(y_ref, dest_ref[k, n], bufs[k], n, sem).wait()
        return carry

    lax.fori_loop(0, tb, start, 0)
    lax.fori_loop(0, tb, wait, 0)
    g = gate_ref[...]
    o_ref[...] = h_ref[...] + (g[:, 0:1] * ylo_ref[...] + g[:, 1:2] * yhi_ref[...])


def moe_combine(dest, gates, h, y):
    n_tok, d = h.shape
    tb = _pick_tile(n_tok, 256)
    return pl.pallas_call(
        functools.partial(_moe_combine_kernel, tb=tb),
        grid=(n_tok // tb,),
        in_specs=[pl.BlockSpec((SUBLANES, tb), lambda i: (0, i), memory_space=pltpu.SMEM),
                  pl.BlockSpec((tb, LANES), lambda i: (i, 0)),
                  pl.BlockSpec((tb, d), lambda i: (i, 0)),
                  pl.BlockSpec(memory_space=pl.ANY)],
        out_specs=pl.BlockSpec((tb, d), lambda i: (i, 0)),
        out_shape=jax.ShapeDtypeStruct((n_tok, d), F32),
        scratch_shapes=[pltpu.VMEM((tb, d), F32), pltpu.VMEM((tb, d), F32), pltpu.SemaphoreType.DMA(())],
        compiler_params=_params("arbitrary"),
        name="moe_combine",
    )(dest, gates, h, y)


def _dsa_swa_layer(h, hn, w_in, kv_norm, w_uk, w_uv, sinks, w_out, rel_bias, bsz, s_len):
    n_tok, d = h.shape
    a_heads = w_uk.shape[0]
    b_heads = sinks.shape[0]
    qa_w, qb_w = a_heads * HEAD_DIM, b_heads * HEAD_DIM
    kv_w = B_KV_HEADS * HEAD_DIM
    qi_w = IDX_HEADS * IDX_DIM
    widths = (qa_w, A_LATENT, qi_w, IDX_DIM, IDX_HEADS, qb_w, kv_w, kv_w)
    off = np.concatenate([[0], np.cumsum(widths)])
    seg = lambda s: w_in[:, int(off[s]):int(off[s + 1])]
    tail_pad = LANES - IDX_DIM - IDX_HEADS
    w_perm = jnp.concatenate([seg(0), seg(5), seg(6), seg(7), seg(1), seg(2), seg(3), seg(4),
                              jnp.zeros((d, tail_pad), w_in.dtype)], axis=1).astype(BF16)
    assert qa_w == qb_w and (qa_w + qb_w) % (2 * kv_w) == 0
    assert (qa_w + qb_w + 2 * kv_w) % A_LATENT == 0 and (qa_w + qb_w + 2 * kv_w + A_LATENT) % qi_w == 0
    kvb_block = (qa_w + qb_w) // (2 * kv_w)
    ckv_block = (qa_w + qb_w + 2 * kv_w) // A_LATENT
    qi_block = (qa_w + qb_w + 2 * kv_w + A_LATENT) // qi_w
    kw_block = (qa_w + qb_w + 2 * kv_w + A_LATENT + qi_w) // LANES

    proj = matmul(hn, w_perm, out_dtype=F32, tn=896)
    proj3 = proj.reshape(bsz, s_len, -1)
    topk = min(TOPK_MAX, s_len // 4)

    packed = latent_pack(proj, ckv_block, kv_norm, RMS_EPS)
    planes = packed.shape[0]
    ckv = jnp.transpose(packed, (1, 0, 2)).reshape(bsz, planes * s_len, LANES)
    idx = dsa_index(proj3, qi_block, kw_block, topk, row_scale=planes)
    q_lat = head_matmul(proj, w_uk.astype(BF16), out_dtype=BF16)
    bias_t = jnp.pad(rel_bias[:, :a_heads].T.astype(F32), ((0, 0), (0, LANES - NUM_BUCKETS)))
    o_lat = dsa_attention(idx, q_lat.reshape(bsz, s_len, a_heads, A_LATENT), ckv, bias_t)
    o_a = head_matmul(o_lat.reshape(n_tok, a_heads * A_LATENT), w_uv.astype(BF16), out_dtype=BF16)

    o_b = swa_attention(proj3, 1, kvb_block, rel_bias.astype(F32), sinks.astype(F32), a_heads)
    o = jnp.concatenate([o_a, o_b.reshape(n_tok, qb_w)], axis=1)
    return matmul(o, w_out.astype(BF16), out_dtype=F32, res=h)


def _diff_layer(h, hn, w_in, lam_vecs, subln, w_out, rel_bias, lambda_init, bsz, s_len):
    n_tok = h.shape[0]
    nheads = w_out.shape[0] // subln.shape[0]
    proj = matmul(hn, w_in.astype(BF16), out_dtype=BF16)
    o = diff_attention(proj.reshape(bsz, s_len, -1), rel_bias.astype(F32), lam_vecs.astype(F32),
                       subln.astype(F32), nheads, lambda_init)
    return matmul(o.reshape(n_tok, -1), w_out.astype(BF16), out_dtype=F32, res=h)


def _moe_layer(h, g, w_router, w_gate, w_up, w_down):
    n_tok, n_experts = h.shape[0], w_router.shape[1]
    xn, comb, sel = norm_router(h, g, w_router, RMS_EPS)
    dest, gates, tiles = moe_plan(sel, comb, n_experts)
    assert (TOP_K * n_tok) % MOE_ROW_TILE == 0
    rows = TOP_K * n_tok + n_experts * MOE_ROW_TILE
    assert rows // MOE_ROW_TILE <= LANES
    xs = moe_scatter(dest, xn, rows)
    up = moe_up(tiles[0], tiles[1], xs, w_gate.astype(BF16), w_up.astype(BF16))
    y = moe_down(tiles[0], tiles[1], up, w_down.astype(BF16))
    return moe_combine(dest, gates, h, y)


def kernel(x, ln_mix, ln_ffn, rel_bias, w_in_ab, kv_norm, w_uk, w_uv, sinks, w_out_ab, w_gate_d, w_up_d,
           w_down_d, w_in_c, lambda_q1, lambda_k1, lambda_q2, lambda_k2, subln, w_out_c, w_router,
           w_gate_e, w_up_e, w_down_e, ln_final):
    bsz, s_len, d = x.shape
    h = x.reshape(bsz * s_len, d)
    for i in range(ln_mix.shape[0]):
        j = i // 2
        hn = rmsnorm(h, ln_mix[i], RMS_EPS, BF16)
        if i % 2 == 0:
            h = _dsa_swa_layer(h, hn, w_in_ab[j], kv_norm[j], w_uk[j], w_uv[j], sinks[j], w_out_ab[j],
                               rel_bias, bsz, s_len)
            hf = rmsnorm(h, ln_ffn[i], RMS_EPS, BF16)
            up = swiglu_up(hf, w_gate_d[j].astype(BF16), w_up_d[j].astype(BF16))
            h = matmul(up, w_down_d[j].astype(BF16), out_dtype=F32, res=h)
        else:
            lambda_init = 0.8 - 0.6 * math.exp(-0.3 * i)
            lam_vecs = jnp.stack([lambda_q1[j], lambda_k1[j], lambda_q2[j], lambda_k2[j]])
            h = _diff_layer(h, hn, w_in_c[j], lam_vecs, subln[j], w_out_c[j], rel_bias, lambda_init,
                            bsz, s_len)
            h = _moe_layer(h, ln_ffn[i], w_router[j], w_gate_e[j], w_up_e[j], w_down_e[j])
    return rmsnorm(h, ln_final, RMS_EPS, x.dtype).reshape(bsz, s_len, d)
```

```python
import functools
import math

import numpy as np
import jax
import jax.numpy as jnp
from jax import lax
from jax.experimental import pallas as pl
from jax.experimental.pallas import tpu as pltpu

F32 = jnp.float32
BF16 = jnp.bfloat16
I32 = jnp.int32

HEAD_DIM = 128
A_LATENT = 512
IDX_HEADS = 16
IDX_DIM = 64
TOPK_MAX = 256
B_KV_HEADS = 2
WINDOW = 128
NUM_BUCKETS = 32
MAX_DISTANCE = 128
TOP_K = 2
RMS_EPS = 1e-6
SUBLN_EPS = 1e-5

LANES = 128
SUBLANES = 8
VMEM_LIMIT_BYTES = 56 * 1024 * 1024
INT_MIN = -(2 ** 31)
NEG_INF = float("-inf")

GATHER_STRIDE = TOPK_MAX + SUBLANES
LATENT_PLANES = A_LATENT // (2 * LANES)
RANKS_PER_STEP = 4


def _bucket_thresholds():
    n = np.arange(0, 4 * MAX_DISTANCE, dtype=np.int64)
    max_exact = NUM_BUCKETS // 2
    nf = np.maximum(n, 1).astype(np.float32)
    large = max_exact + (
        np.log(nf / np.float32(max_exact)) / np.float32(math.log(MAX_DISTANCE / max_exact))
        * np.float32(NUM_BUCKETS - max_exact)
    ).astype(np.int32)
    large = np.minimum(large, NUM_BUCKETS - 1)
    bucket = np.where(n < max_exact, n, large)
    assert np.all(np.diff(bucket) >= 0) and bucket[-1] == NUM_BUCKETS - 1
    return tuple(int(np.argmax(bucket >= k)) for k in range(1, NUM_BUCKETS))


BUCKET_THRESHOLDS = _bucket_thresholds()
FAR_DISTANCE = BUCKET_THRESHOLDS[-1]


def _bias_from_distance(dist, table_entry):
    val = jnp.where(dist >= BUCKET_THRESHOLDS[0], table_entry(1), table_entry(0))
    for k in range(2, NUM_BUCKETS):
        val = jnp.where(dist >= BUCKET_THRESHOLDS[k - 1], table_entry(k), val)
    return val


def _pick_tile(dim, target, align=LANES):
    if dim <= target:
        return dim
    t = (target // align) * align
    while t >= align:
        if dim % t == 0:
            return t
        t -= align
    return dim


def _params(*semantics):
    return pltpu.CompilerParams(dimension_semantics=semantics, vmem_limit_bytes=VMEM_LIMIT_BYTES)


def _rmsnorm_kernel(x_ref, g_ref, o_ref, *, eps):
    x = x_ref[...].astype(F32)
    y = x * lax.rsqrt(jnp.mean(x * x, axis=-1, keepdims=True) + eps)
    o_ref[...] = (y * g_ref[...].astype(F32)).astype(o_ref.dtype)


def rmsnorm(x2d, g, eps, out_dtype):
    m, d = x2d.shape
    tm = _pick_tile(m, 256, SUBLANES)
    return pl.pallas_call(
        functools.partial(_rmsnorm_kernel, eps=eps),
        grid=(m // tm,),
        in_specs=[pl.BlockSpec((tm, d), lambda i: (i, 0)), pl.BlockSpec((1, d), lambda i: (0, 0))],
        out_specs=pl.BlockSpec((tm, d), lambda i: (i, 0)),
        out_shape=jax.ShapeDtypeStruct((m, d), out_dtype),
        compiler_params=_params("parallel"),
        name="rmsnorm",
    )(x2d, g.reshape(1, d))


def _matmul_kernel(*refs, nk, has_res, scale_col):
    a_ref, w_ref = refs[0], refs[1]
    pos = 2
    res_ref = scale_ref = None
    if has_res:
        res_ref = refs[pos]
        pos += 1
    if scale_col is not None:
        scale_ref = refs[pos]
        pos += 1
    o_ref = refs[pos]
    k = pl.program_id(2)

    def finish(y):
        if scale_ref is not None:
            y = y * scale_ref[:, scale_col:scale_col + 1]
        if res_ref is not None:
            y = res_ref[...].astype(F32) + y
        o_ref[...] = y.astype(o_ref.dtype)

    part = jnp.dot(a_ref[...], w_ref[...], preferred_element_type=F32)
    if nk == 1:
        finish(part)
        return
    acc_ref = refs[pos + 1]

    @pl.when(k == 0)
    def _():
        acc_ref[...] = part

    @pl.when(k > 0)
    def _():
        acc_ref[...] += part

    @pl.when(k == nk - 1)
    def _():
        finish(acc_ref[...])


def matmul(a, w, *, out_dtype, res=None, row_scale=None, scale_col=None, expert=None,
           tm=1024, tn=512, tk=4096):
    m, kdim = a.shape
    n = w.shape[-1]
    tm, tn, tk = _pick_tile(m, tm, SUBLANES), _pick_tile(n, tn), _pick_tile(kdim, tk)
    nk = kdim // tk
    if expert is None:
        w_spec = pl.BlockSpec((tk, tn), lambda i, j, k: (k, j))
    else:
        w_spec = pl.BlockSpec((None, tk, tn), lambda i, j, k: (expert, k, j))
    in_specs = [pl.BlockSpec((tm, tk), lambda i, j, k: (i, k)), w_spec]
    args = [a, w]
    if res is not None:
        in_specs.append(pl.BlockSpec((tm, tn), lambda i, j, k: (i, j)))
        args.append(res)
    if row_scale is not None:
        in_specs.append(pl.BlockSpec((tm, row_scale.shape[1]), lambda i, j, k: (i, 0)))
        args.append(row_scale)
    return pl.pallas_call(
        functools.partial(_matmul_kernel, nk=nk, has_res=res is not None,
                          scale_col=scale_col if row_scale is not None else None),
        grid=(m // tm, n // tn, nk),
        in_specs=in_specs,
        out_specs=pl.BlockSpec((tm, tn), lambda i, j, k: (i, j)),
        out_shape=jax.ShapeDtypeStruct((m, n), out_dtype),
        scratch_shapes=[pltpu.VMEM((tm, tn), F32)] if nk > 1 else [],
        compiler_params=_params("parallel", "parallel", "arbitrary"),
        name="matmul",
    )(*args)


def _swiglu_up_kernel(a_ref, wg_ref, wu_ref, o_ref, *acc_refs, nk):
    k = pl.program_id(2)

    def finish(g, u):
        silu = g * (1.0 / (1.0 + jnp.exp(-g)))
        o_ref[...] = (silu * u).astype(o_ref.dtype)

    a = a_ref[...]
    gate = jnp.dot(a, wg_ref[...], preferred_element_type=F32)
    up = jnp.dot(a, wu_ref[...], preferred_element_type=F32)
    if nk == 1:
        finish(gate, up)
        return
    accg_ref, accu_ref = acc_refs

    @pl.when(k == 0)
    def _():
        accg_ref[...] = gate
        accu_ref[...] = up

    @pl.when(k > 0)
    def _():
        accg_ref[...] += gate
        accu_ref[...] += up

    @pl.when(k == nk - 1)
    def _():
        finish(accg_ref[...], accu_ref[...])


def swiglu_up(a, wg, wu, *, expert=None, tm=1024, tn=512, tk=4096):
    m, kdim = a.shape
    f = wg.shape[-1]
    tm, tn, tk = _pick_tile(m, tm, SUBLANES), _pick_tile(f, tn), _pick_tile(kdim, tk)
    nk = kdim // tk
    if expert is None:
        w_spec = pl.BlockSpec((tk, tn), lambda i, j, k: (k, j))
    else:
        w_spec = pl.BlockSpec((None, tk, tn), lambda i, j, k: (expert, k, j))
    return pl.pallas_call(
        functools.partial(_swiglu_up_kernel, nk=nk),
        grid=(m // tm, f // tn, nk),
        in_specs=[pl.BlockSpec((tm, tk), lambda i, j, k: (i, k)), w_spec, w_spec],
        out_specs=pl.BlockSpec((tm, tn), lambda i, j, k: (i, j)),
        out_shape=jax.ShapeDtypeStruct((m, f), BF16),
        scratch_shapes=[pltpu.VMEM((tm, tn), F32), pltpu.VMEM((tm, tn), F32)] if nk > 1 else [],
        compiler_params=_params("parallel", "parallel", "arbitrary"),
        name="swiglu_up",
    )(a, wg, wu)


def _head_matmul_kernel(a_ref, w_ref, o_ref):
    o_ref[...] = jnp.dot(a_ref[...].astype(BF16), w_ref[...], preferred_element_type=F32).astype(o_ref.dtype)


def head_matmul(a, w3, *, out_dtype, tm=1024):
    m = a.shape[0]
    nheads, ka, nb = w3.shape
    tm = _pick_tile(m, tm, SUBLANES)
    return pl.pallas_call(
        _head_matmul_kernel,
        grid=(m // tm, nheads),
        in_specs=[pl.BlockSpec((tm, ka), lambda i, h: (i, h)),
                  pl.BlockSpec((None, ka, nb), lambda i, h: (h, 0, 0))],
        out_specs=pl.BlockSpec((tm, nb), lambda i, h: (i, h)),
        out_shape=jax.ShapeDtypeStruct((m, nheads * nb), out_dtype),
        compiler_params=_params("parallel", "parallel"),
        name="head_matmul",
    )(a, w3)


def _bits(x):
    return lax.bitcast_convert_type(x, I32)


def _latent_pack_kernel(c_ref, g_ref, o_ref, *, eps):
    x = c_ref[...]
    y = x * lax.rsqrt(jnp.mean(x * x, axis=-1, keepdims=True) + eps) * g_ref[...]
    yb = _bits(y.astype(BF16).astype(F32))
    for r in range(A_LATENT // (2 * LANES)):
        lo = lax.shift_right_logical(yb[:, (2 * r) * LANES:(2 * r + 1) * LANES], 16)
        hi = yb[:, (2 * r + 1) * LANES:(2 * r + 2) * LANES] & jnp.int32(-65536)
        o_ref[r] = lo | hi


def latent_pack(proj, col_block, g, eps):
    m = proj.shape[0]
    tm = _pick_tile(m, 512, SUBLANES)
    planes = A_LATENT // (2 * LANES)
    return pl.pallas_call(
        functools.partial(_latent_pack_kernel, eps=eps),
        grid=(m // tm,),
        in_specs=[pl.BlockSpec((tm, A_LATENT), lambda i: (i, col_block)),
                  pl.BlockSpec((1, A_LATENT), lambda i: (0, 0))],
        out_specs=pl.BlockSpec((planes, tm, LANES), lambda i: (0, i, 0)),
        out_shape=jax.ShapeDtypeStruct((planes, m, LANES), I32),
        compiler_params=_params("parallel"),
        name="latent_pack",
    )(proj, g.reshape(1, A_LATENT))


def _dsa_index_kernel(qi_ref, kw_ref, kwq_ref, idx_ref, qt_ref, key_ref, rank_ref, acc_ref, run_ref, bnd_ref,
                      *, tq, topk, row_scale):
    ch = tq
    i = pl.program_id(1)
    nchunks = i + 1

    q_t = qi_ref[0].T
    for h in range(IDX_HEADS):
        qt_ref[:, h * tq:(h + 1) * tq] = q_t[h * IDX_DIM:(h + 1) * IDX_DIM, :].astype(BF16)
    w_t = kwq_ref[0].T[IDX_DIM:IDX_DIM + IDX_HEADS, :] * (IDX_DIM ** -0.5 * IDX_HEADS ** -0.5)
    t_row = i * tq + lax.broadcasted_iota(I32, (1, tq), 1)
    s_iota = lax.broadcasted_iota(I32, (ch, tq), 0)

    def chunk_start(c):
        return pl.multiple_of(c * ch, ch)

    def score_chunk(c, carry):
        r0 = chunk_start(c)
        kc = kw_ref[0, pl.ds(r0, ch), :][:, :IDX_DIM].astype(BF16)
        dots = jnp.dot(kc, qt_ref[...], preferred_element_type=F32)
        sc = jnp.zeros((ch, tq), F32)
        for h in range(IDX_HEADS):
            sc = sc + jnp.maximum(dots[:, h * tq:(h + 1) * tq], 0.0) * w_t[h:h + 1, :]
        bits = _bits(sc)
        key = bits ^ ((bits >> 31) & jnp.int32(0x7FFFFFFF))
        key_ref[pl.ds(r0, ch), :] = jnp.where(r0 + s_iota <= t_row, key, jnp.int32(INT_MIN))
        return carry

    lax.fori_loop(0, nchunks, score_chunk, 0)

    def column_sum(m):
        return m.reshape(ch // SUBLANES, SUBLANES, tq).sum(axis=0)

    def count(preds):
        def body(c, parts):
            k = key_ref[pl.ds(chunk_start(c), ch), :]
            return tuple(part + column_sum(jnp.where(pred(k), 1, 0).astype(I32)) for part, pred in zip(parts, preds))
        parts = lax.fori_loop(0, nchunks, body, tuple(jnp.zeros((SUBLANES, tq), I32) for _ in preds))
        return tuple(part.sum(axis=0, keepdims=True) for part in parts)

    def at_least(cand):
        return lambda k: k >= cand

    thr = jnp.where(count((at_least(0),))[0] >= topk, 0, INT_MIN).astype(I32)

    def two_bits(p, thr):
        hi = lax.shift_left(jnp.int32(1), 30 - 2 * p)
        lo = lax.shift_left(jnp.int32(1), 29 - 2 * p)
        c11, c10, c01 = thr | hi | lo, thr | hi, thr | lo
        n11, n10, n01 = count((at_least(c11), at_least(c10), at_least(c01)))
        return jnp.where(n11 >= topk, c11, jnp.where(n10 >= topk, c10, jnp.where(n01 >= topk, c01, thr)))

    thr = lax.fori_loop(0, 15, two_bits, thr)
    thr = jnp.where(count((at_least(thr | 1),))[0] >= topk, thr | 1, thr)
    n_gt, n_ge = count((lambda k: k > thr, lambda k: (k >= thr) & (k > INT_MIN)))
    need = (topk - n_gt).astype(F32)
    has_ties = jnp.max(n_ge) > topk

    tri = (lax.broadcasted_iota(I32, (ch, ch), 0) >= lax.broadcasted_iota(I32, (ch, ch), 1)).astype(BF16)

    def select_chunk(with_ties, c, carry):
        run, eq_run = carry
        r0 = chunk_start(c)
        k = key_ref[pl.ds(r0, ch), :]
        if with_ties:
            eq = jnp.where(k == thr, jnp.where(k > INT_MIN, 1.0, 0.0), 0.0)
            eq_cum = jnp.dot(tri, eq.astype(BF16), preferred_element_type=F32) + eq_run
            sel = jnp.where(k > thr, 1.0, jnp.where(eq_cum <= need, eq, 0.0))
            eq_run = eq_cum[ch - 1:ch, :]
        else:
            sel = jnp.where(k >= thr, jnp.where(k > INT_MIN, 1.0, 0.0), 0.0)
        rank_ref[pl.ds(r0, ch), :] = sel * jnp.dot(tri, sel.astype(BF16), preferred_element_type=F32)
        n_sel = column_sum(sel).sum(axis=0, keepdims=True)
        run_ref[c] = run
        bnd_ref[0, c] = jnp.min(run).astype(I32)
        bnd_ref[1, c] = jnp.max(run).astype(I32)
        bnd_ref[2, c] = jnp.max(n_sel).astype(I32)
        return run + n_sel, eq_run

    zero_row = jnp.zeros((1, tq), F32)

    @pl.when(has_ties)
    def _():
        lax.fori_loop(0, nchunks, functools.partial(select_chunk, True), (zero_row, zero_row))

    @pl.when(jnp.logical_not(has_ties))
    def _():
        lax.fori_loop(0, nchunks, functools.partial(select_chunk, False), (zero_row, zero_row))

    acc_ref[...] = jnp.zeros_like(acc_ref)
    sub_iota = lax.broadcasted_iota(I32, (SUBLANES, tq), 0)
    last_tile = topk // SUBLANES - 1

    def compact_chunk(c, carry):
        r0 = chunk_start(c)
        run_lo, run_hi = bnd_ref[0, c], bnd_ref[1, c]
        run = run_ref[c]
        key_index_1 = (r0 + 1 + s_iota).astype(F32)

        def place_ranks(g, carry):
            rank = rank_ref[pl.ds(r0, ch), :]
            first = g * RANKS_PER_STEP
            slots, keys = [], []
            for u in range(RANKS_PER_STEP):
                r = (first + u).astype(F32)
                hit = jnp.where(rank == r + 1.0, key_index_1, 0.0)
                found = column_sum(hit).sum(axis=0, keepdims=True)
                slots.append(jnp.where(found > 0.0, run + r, -1.0))
                keys.append(found - 1.0)

            def place_tile(t8, carry):
                j0 = pl.multiple_of(t8 * SUBLANES, SUBLANES)
                rows = (j0 + sub_iota).astype(F32)
                cur = acc_ref[pl.ds(j0, SUBLANES), :]
                for slot, key in zip(slots, keys):
                    cur = jnp.where(rows == slot, key, cur)
                acc_ref[pl.ds(j0, SUBLANES), :] = cur
                return carry

            hi_tile = jnp.minimum((run_hi + first + (RANKS_PER_STEP - 1)) >> 3, last_tile)
            lax.fori_loop((run_lo + first) >> 3, hi_tile + 1, place_tile, 0)
            return carry

        steps = (bnd_ref[2, c] + (RANKS_PER_STEP - 1)) >> (RANKS_PER_STEP.bit_length() - 1)
        lax.fori_loop(0, steps, place_ranks, 0)
        return carry

    lax.fori_loop(0, nchunks, compact_chunk, 0)
    idx_ref[0] = (acc_ref[...] * float(row_scale)).T.astype(I32)


def dsa_index(proj3, qi_block, kw_block, topk, row_scale=1, tq=128):
    bsz, s_len, _ = proj3.shape
    tq = min(tq, s_len)
    nq = s_len // tq
    return pl.pallas_call(
        functools.partial(_dsa_index_kernel, tq=tq, topk=topk, row_scale=row_scale),
        grid=(bsz, nq),
        in_specs=[pl.BlockSpec((1, tq, IDX_HEADS * IDX_DIM), lambda b, i: (b, i, qi_block)),
                  pl.BlockSpec((1, s_len, LANES), lambda b, i: (b, 0, kw_block)),
                  pl.BlockSpec((1, tq, LANES), lambda b, i: (b, i, kw_block))],
        out_specs=pl.BlockSpec((1, tq, topk), lambda b, i: (b, i, 0)),
        out_shape=jax.ShapeDtypeStruct((bsz, s_len, topk), I32),
        scratch_shapes=[pltpu.VMEM((IDX_DIM, IDX_HEADS * tq), BF16),
                        pltpu.VMEM((s_len, tq), I32),
                        pltpu.VMEM((s_len, tq), F32),
                        pltpu.VMEM((topk, tq), F32),
                        pltpu.VMEM((nq, 1, tq), F32),
                        pltpu.SMEM((3, nq), I32)],
        compiler_params=_params("parallel", "arbitrary"),
        name="dsa_index",
    )(proj3, proj3, proj3)


def _dsa_attn_kernel(idx_s_ref, idx_v_ref, ql_ref, ckv_ref, tab_ref, o_ref, *tile_refs, tq, topk, nheads):
    i = pl.program_id(1)
    planes = LATENT_PLANES
    plane_shift = planes.bit_length() - 1
    assert planes == 1 << plane_shift
    lane_iota = lax.broadcasted_iota(I32, (nheads, topk), 1)
    scale = HEAD_DIM ** -0.5

    def gather(q, tile_ref, lo, hi):
        for mi in range(lo, hi):
            row = pl.multiple_of(idx_s_ref[0, q, mi], planes)
            tile_ref[pl.ds(mi, planes, stride=GATHER_STRIDE), :] = ckv_ref[0, pl.ds(row, planes), :]

    def attend(queries, tiles):
        nq = len(queries)
        xs = [[] for _ in queries]
        for r in range(planes):
            for n in range(nq):
                w = tiles[n][r * GATHER_STRIDE:r * GATHER_STRIDE + topk, :]
                xs[n].append(lax.bitcast_convert_type(lax.shift_left(w, 16), F32).astype(BF16))
                xs[n].append(lax.bitcast_convert_type(w & jnp.int32(-65536), F32).astype(BF16))
            yield
        logits = []
        for n, q in enumerate(queries):
            ql = ql_ref[0, q]
            acc = jnp.zeros((nheads, topk), F32)
            for j, x in enumerate(xs[n]):
                acc = acc + lax.dot_general(ql[:, j * LANES:(j + 1) * LANES], x, (((1,), (1,)), ((), ())),
                                            preferred_element_type=F32)
            logits.append(acc)
        yield
        bias = []
        for q in queries:
            dist = jnp.broadcast_to(i * tq + q - (idx_v_ref[0, q] >> plane_shift), (nheads, topk))
            bias.append(_bias_from_distance(dist, lambda k: tab_ref[:, k:k + 1]))
        yield
        es, inv = [], []
        for n, q in enumerate(queries):
            z = jnp.where(lane_iota < jnp.minimum(i * tq + q + 1, topk), logits[n] * scale + bias[n], NEG_INF)
            e = jnp.exp(z - jnp.max(z, axis=-1, keepdims=True))
            inv.append(1.0 / jnp.sum(e, axis=-1, keepdims=True))
            es.append(e.astype(BF16))
        yield
        for j in range(2 * planes):
            for n, q in enumerate(queries):
                o = jnp.dot(es[n], xs[n][j], preferred_element_type=F32) * inv[n]
                o_ref[0, q, :, j * LANES:(j + 1) * LANES] = o.astype(o_ref.dtype)
            if j % 2 == 1:
                yield

    n_stages = 2 * planes + 3
    rows_per_stage = topk // n_stages

    def attend_while_gathering(queries, tiles, next_queries, next_tiles):
        stages = attend(queries, tiles)
        for s in range(n_stages):
            next(stages)
            hi = topk if s == n_stages - 1 else (s + 1) * rows_per_stage
            for q, tile_ref in zip(next_queries, next_tiles):
                gather(q, tile_ref, s * rows_per_stage, hi)

    half = len(tile_refs) // 2
    first, second = tile_refs[:half], tile_refs[half:]
    for n in range(half):
        gather(n, first[n], 0, topk)

    def query_group(qg, carry):
        q0 = 2 * half * qg
        qa = [q0 + n for n in range(half)]
        qb = [q0 + half + n for n in range(half)]
        qc = [jnp.minimum(q0 + 2 * half + n, tq - 1) for n in range(half)]
        attend_while_gathering(qa, first, qb, second)
        attend_while_gathering(qb, second, qc, first)
        return carry

    lax.fori_loop(0, tq // (2 * half), query_group, 0)


def dsa_attention(idx, q_lat, ckv_packed, bias_t, tq=64):
    bsz, s_len, topk = idx.shape
    nheads = q_lat.shape[2]
    tq = min(tq, s_len)
    planes = LATENT_PLANES
    return pl.pallas_call(
        functools.partial(_dsa_attn_kernel, tq=tq, topk=topk, nheads=nheads),
        grid=(bsz, s_len // tq),
        in_specs=[pl.BlockSpec((1, tq, topk), lambda b, i: (b, i, 0), memory_space=pltpu.SMEM),
                  pl.BlockSpec((1, tq, 1, topk), lambda b, i: (b, i, 0, 0)),
                  pl.BlockSpec((1, tq, nheads, A_LATENT), lambda b, i: (b, i, 0, 0)),
                  pl.BlockSpec((1, planes * s_len, LANES), lambda b, i: (b, 0, 0)),
                  pl.BlockSpec((nheads, LANES), lambda b, i: (0, 0))],
        out_specs=pl.BlockSpec((1, tq, nheads, A_LATENT), lambda b, i: (b, i, 0, 0)),
        out_shape=jax.ShapeDtypeStruct((bsz, s_len, nheads, A_LATENT), BF16),
        scratch_shapes=[pltpu.VMEM((planes * GATHER_STRIDE, LANES), I32)] * 4,
        compiler_params=_params("parallel", "arbitrary"),
        name="dsa_attention",
    )(idx, idx.reshape(bsz, s_len, 1, topk), q_lat, ckv_packed, bias_t)


def _swa_kernel(tab_ref, sink_ref, q_ref, kvp_ref, kvc_ref, o_ref, band_ref, *, nheads, slot0):
    w = WINDOW
    group = nheads // B_KV_HEADS
    b, n = pl.program_id(0), pl.program_id(1)
    rel = (lax.broadcasted_iota(I32, (w, 2 * w), 0) + w) - lax.broadcasted_iota(I32, (w, 2 * w), 1)

    @pl.when((b == 0) & (n == 0))
    def _():
        for hb in range(nheads):
            band_ref[hb] = _bias_from_distance(rel, lambda k: tab_ref[k, slot0 + hb])

    s_global = n * w - w + lax.broadcasted_iota(I32, (w, 2 * w), 1)
    mask = (rel >= 0) & (rel < w) & (s_global >= 0)
    scale = HEAD_DIM ** -0.5
    kv_width = B_KV_HEADS * HEAD_DIM
    for kh in range(B_KV_HEADS):
        ks = slice(kh * HEAD_DIM, (kh + 1) * HEAD_DIM)
        vs = slice(kv_width + kh * HEAD_DIM, kv_width + (kh + 1) * HEAD_DIM)
        kk = jnp.concatenate([kvp_ref[0][:, ks], kvc_ref[0][:, ks]], axis=0).astype(BF16)
        vv = jnp.concatenate([kvp_ref[0][:, vs], kvc_ref[0][:, vs]], axis=0).astype(BF16)
        for g in range(group):
            hb = kh * group + g
            cols = slice(hb * HEAD_DIM, (hb + 1) * HEAD_DIM)
            q = q_ref[0][:, cols].astype(BF16)
            z = lax.dot_general(q, kk, (((1,), (1,)), ((), ())), preferred_element_type=F32) * scale + band_ref[hb]
            z = jnp.where(mask, z, NEG_INF)
            sink = sink_ref[hb]
            m = jnp.maximum(jnp.max(z, axis=-1, keepdims=True), sink)
            e = jnp.exp(z - m)
            den = jnp.sum(e, axis=-1, keepdims=True) + jnp.exp(sink - m)
            o_ref[0, :, cols] = jnp.dot((e / den).astype(BF16), vv, preferred_element_type=F32).astype(o_ref.dtype)


def swa_attention(proj3, q_block, kv_block, rel_bias, sinks, slot0):
    bsz, s_len, _ = proj3.shape
    nheads = sinks.shape[0]
    qw = nheads * HEAD_DIM
    kvw = 2 * B_KV_HEADS * HEAD_DIM
    w = WINDOW
    smem = pl.BlockSpec(memory_space=pltpu.SMEM)
    return pl.pallas_call(
        functools.partial(_swa_kernel, nheads=nheads, slot0=slot0),
        grid=(bsz, s_len // w),
        in_specs=[smem, smem,
                  pl.BlockSpec((1, w, qw), lambda b, n: (b, n, q_block)),
                  pl.BlockSpec((1, w, kvw), lambda b, n: (b, jnp.maximum(n - 1, 0), kv_block)),
                  pl.BlockSpec((1, w, kvw), lambda b, n: (b, n, kv_block))],
        out_specs=pl.BlockSpec((1, w, qw), lambda b, n: (b, n, 0)),
        out_shape=jax.ShapeDtypeStruct((bsz, s_len, qw), BF16),
        scratch_shapes=[pltpu.VMEM((nheads, w, 2 * w), F32)],
        compiler_params=_params("arbitrary", "arbitrary"),
        name="swa_attention",
    )(rel_bias, sinks, proj3, proj3, proj3)


def _diff_attn_kernel(tab_ref, lam_ref, subln_ref, q1_ref, q2_ref, k1_ref, k2_ref, v_ref, o_ref,
                      band_ref, m_ref, l_ref, acc_ref, *, tq, nheads, lambda_init):
    h, qi = pl.program_id(1), pl.program_id(2)
    ii = lax.broadcasted_iota(I32, (tq, tq), 0)
    jj = lax.broadcasted_iota(I32, (tq, tq), 1)
    log2e = 1.4426950408889634
    qk_scale = HEAD_DIM ** -0.5 * log2e
    last = NUM_BUCKETS - 1
    ntile = tq // LANES
    vd = acc_ref.shape[-1]

    @pl.when(qi == 0)
    def _():
        for m in range(2):
            slot = m * nheads + h
            far = tab_ref[last, slot]
            band_ref[m, 0] = (_bias_from_distance(ii - jj, lambda k: tab_ref[k, slot]) - far) * log2e
            band_ref[m, 1] = (_bias_from_distance(ii - jj + tq, lambda k: tab_ref[k, slot]) - far) * log2e

    m_ref[...] = jnp.full_like(m_ref, NEG_INF)
    l_ref[...] = jnp.zeros_like(l_ref)
    acc_ref[...] = jnp.zeros_like(acc_ref)
    qs = (q1_ref[0], q2_ref[0])
    k_refs = (k1_ref, k2_ref)

    def block(kj, band_idx, causal):
        r0 = pl.multiple_of(kj * tq, tq)
        v = v_ref[0, pl.ds(r0, tq), :]
        for m in range(2):
            k = k_refs[m][0, pl.ds(r0, tq), :]
            z = lax.dot_general(qs[m], k, (((1,), (1,)), ((), ())), preferred_element_type=F32) * qk_scale
            if band_idx is not None:
                z = z + band_ref[m, band_idx]
            if causal:
                z = jnp.where(jj <= ii, z, NEG_INF)
            zt = [z[:, j * LANES:(j + 1) * LANES] for j in range(ntile)]
            m_old = m_ref[m]
            m_new = jnp.maximum(m_old, jnp.max(functools.reduce(jnp.maximum, zt), axis=-1, keepdims=True))
            alpha = jnp.exp2(m_old - m_new)
            es = [jnp.exp2(t - m_new) for t in zt]
            l_ref[m] = alpha * l_ref[m] + functools.reduce(jnp.add, es)
            pv = jnp.dot(jnp.concatenate([e.astype(BF16) for e in es], axis=1), v, preferred_element_type=F32)
            for c in range(vd // LANES):
                cols = slice(c * LANES, (c + 1) * LANES)
                acc_ref[m, :, cols] = alpha * acc_ref[m, :, cols] + pv[:, cols]
            m_ref[m] = m_new

    def far_block(kj, carry):
        block(kj, None, causal=False)
        return carry

    lax.fori_loop(0, jnp.maximum(qi - 1, 0), far_block, 0)

    @pl.when(qi >= 1)
    def _():
        block(qi - 1, 1, causal=False)

    block(qi, 0, causal=True)

    lam = (jnp.exp(jnp.sum(lam_ref[0:1, :] * lam_ref[1:2, :], axis=-1, keepdims=True))
           - jnp.exp(jnp.sum(lam_ref[2:3, :] * lam_ref[3:4, :], axis=-1, keepdims=True)) + lambda_init)
    outs = [acc_ref[m] / jnp.sum(l_ref[m], axis=-1, keepdims=True) for m in range(2)]
    o = outs[0] - lam * outs[1]
    y = o * lax.rsqrt(jnp.mean(o * o, axis=-1, keepdims=True) + SUBLN_EPS)
    o_ref[0] = ((y * subln_ref[...]) * (1.0 - lambda_init)).astype(o_ref.dtype)


def diff_attention(proj3, rel_bias, lam_vecs, subln, nheads, lambda_init, tq=512):
    bsz, s_len, _ = proj3.shape
    tq = min(tq, s_len)
    assert tq > FAR_DISTANCE
    vd = 2 * HEAD_DIM
    smem = pl.BlockSpec(memory_space=pltpu.SMEM)
    q_spec = [pl.BlockSpec((1, tq, HEAD_DIM), functools.partial(lambda m, b, h, i: (b, i, m * nheads + h), m))
              for m in range(2)]
    k_spec = [pl.BlockSpec((1, s_len, HEAD_DIM),
                           functools.partial(lambda m, b, h, i: (b, 0, (2 + m) * nheads + h), m))
              for m in range(2)]
    v_spec = pl.BlockSpec((1, s_len, vd), lambda b, h, i: (b, 0, 2 * nheads + h))
    return pl.pallas_call(
        functools.partial(_diff_attn_kernel, tq=tq, nheads=nheads, lambda_init=lambda_init),
        grid=(bsz, nheads, s_len // tq),
        in_specs=[smem, pl.BlockSpec((4, HEAD_DIM), lambda b, h, i: (0, 0)),
                  pl.BlockSpec((1, vd), lambda b, h, i: (0, 0)),
                  q_spec[0], q_spec[1], k_spec[0], k_spec[1], v_spec],
        out_specs=pl.BlockSpec((1, tq, vd), lambda b, h, i: (b, i, h)),
        out_shape=jax.ShapeDtypeStruct((bsz, s_len, nheads * vd), BF16),
        scratch_shapes=[pltpu.VMEM((2, 2, tq, tq), F32), pltpu.VMEM((2, tq, LANES), F32),
                        pltpu.VMEM((2, tq, LANES), F32), pltpu.VMEM((2, tq, vd), F32)],
        compiler_params=_params("arbitrary", "arbitrary", "arbitrary"),
        name="diff_attention",
    )(rel_bias, lam_vecs, subln.reshape(1, vd), proj3, proj3, proj3, proj3, proj3)


def _norm_router_kernel(x_ref, g_ref, wr_ref, xn_ref, comb_ref, sel_ref, *, eps, n_experts):
    x = x_ref[...]
    y = x * lax.rsqrt(jnp.mean(x * x, axis=-1, keepdims=True) + eps) * g_ref[...]
    xn_ref[...] = y
    logits = jnp.dot(y, wr_ref[...], preferred_element_type=F32, precision=lax.Precision.HIGHEST)
    lane = lax.broadcasted_iota(I32, logits.shape, 1)
    z = jnp.where(lane < n_experts, logits, NEG_INF)
    v1 = jnp.max(z, axis=-1, keepdims=True)
    i1 = jnp.min(jnp.where(z == v1, lane, LANES), axis=-1, keepdims=True)
    z2 = jnp.where(lane == i1, NEG_INF, z)
    v2 = jnp.max(z2, axis=-1, keepdims=True)
    i2 = jnp.min(jnp.where(z2 == v2, lane, LANES), axis=-1, keepdims=True)
    e2 = jnp.exp(v2 - v1)
    den = 1.0 + e2
    comb_ref[...] = jnp.where(lane == i1, 1.0 / den, 0.0) + jnp.where(lane == i2, e2 / den, 0.0)
    sel_ref[...] = jnp.where(lane == i1, 1.0, 0.0) + jnp.where(lane == i2, 1.0, 0.0)


def norm_router(x2d, g, w_router, eps):
    m, d = x2d.shape
    n_experts = w_router.shape[1]
    tm = _pick_tile(m, 256, SUBLANES)
    wr = jnp.pad(w_router.astype(F32), ((0, 0), (0, LANES - n_experts)))
    lane_spec = pl.BlockSpec((tm, LANES), lambda i: (i, 0))
    return pl.pallas_call(
        functools.partial(_norm_router_kernel, eps=eps, n_experts=n_experts),
        grid=(m // tm,),
        in_specs=[pl.BlockSpec((tm, d), lambda i: (i, 0)), pl.BlockSpec((1, d), lambda i: (0, 0)),
                  pl.BlockSpec((d, LANES), lambda i: (0, 0))],
        out_specs=[pl.BlockSpec((tm, d), lambda i: (i, 0)), lane_spec, lane_spec],
        out_shape=[jax.ShapeDtypeStruct((m, d), F32), jax.ShapeDtypeStruct((m, LANES), F32),
                   jax.ShapeDtypeStruct((m, LANES), F32)],
        compiler_params=_params("parallel"),
        name="norm_router",
    )(x2d, g.reshape(1, d), wr)


MOE_ROW_TILE = 512
PLAN_ROWS = 256


def _moe_plan_kernel(sel_ref, comb_ref, dest_ref, gate_ref, tile_ref, *, n_experts, n_tok):
    rb = PLAN_ROWS
    nblk = n_tok // rb
    shift = MOE_ROW_TILE.bit_length() - 1
    lane = lax.broadcasted_iota(I32, (1, LANES), 1)

    def count_block(b, part):
        m = sel_ref[pl.ds(pl.multiple_of(b * rb, rb), rb), :]
        return part + m.reshape(rb // SUBLANES, SUBLANES, LANES).sum(axis=0)

    cnt = lax.fori_loop(0, nblk, count_block, jnp.zeros((SUBLANES, LANES), F32)).sum(axis=0, keepdims=True)
    padded = (((cnt.astype(I32) + (MOE_ROW_TILE - 1)) >> shift) << shift)
    start = jnp.zeros((1, LANES), I32)
    tile_expert = jnp.zeros((1, LANES), I32)
    for e in range(n_experts):
        begin = jnp.sum(jnp.where(lane < e, padded, 0), axis=-1, keepdims=True)
        end = jnp.sum(jnp.where(lane <= e, padded, 0), axis=-1, keepdims=True)
        start = start + jnp.where(lane == e, begin, 0)
        tile_expert = tile_expert + jnp.where(lane * MOE_ROW_TILE >= end, 1, 0)
    total = jnp.sum(jnp.where(lane < n_experts, padded, 0), axis=-1, keepdims=True)
    tile_ref[...] = jnp.zeros_like(tile_ref)
    tile_ref[0:1, :] = jnp.minimum(tile_expert, n_experts - 1)
    tile_ref[1:2, :] = jnp.where(lane * MOE_ROW_TILE < total, 1, 0)

    start_f = start.astype(F32)
    strict_lower = (lax.broadcasted_iota(I32, (rb, rb), 0) > lax.broadcasted_iota(I32, (rb, rb), 1)).astype(BF16)
    lane_b = lax.broadcasted_iota(I32, (rb, LANES), 1)
    big = float(2 ** 30)

    def rank_block(b, run):
        r0 = pl.multiple_of(b * rb, rb)
        m = sel_ref[pl.ds(r0, rb), :]
        before = jnp.dot(strict_lower, m.astype(BF16), preferred_element_type=F32) + run
        row = start_f + before
        lo = jnp.min(jnp.where(m > 0.0, row, big), axis=-1, keepdims=True)
        hi = jnp.max(jnp.where(m > 0.0, row, -1.0), axis=-1, keepdims=True)
        comb = comb_ref[pl.ds(r0, rb), :]
        g_lo = jnp.sum(jnp.where((m > 0.0) & (row == lo), comb, 0.0), axis=-1, keepdims=True)
        g_hi = jnp.sum(jnp.where((m > 0.0) & (row == hi), comb, 0.0), axis=-1, keepdims=True)
        gate_ref[pl.ds(r0, rb), :] = jnp.where(lane_b == 0, g_lo, jnp.where(lane_b == 1, g_hi, 0.0))
        rows_t = jnp.where(lane_b == 0, lo, jnp.where(lane_b == 1, hi, 0.0)).T
        dest_ref[:, pl.ds(r0, rb)] = rows_t[0:SUBLANES, :].astype(I32)
        return before[rb - 1:rb, :] + m[rb - 1:rb, :]

    lax.fori_loop(0, nblk, rank_block, jnp.zeros((1, LANES), F32))


def moe_plan(sel, comb, n_experts):
    n_tok = sel.shape[0]
    assert n_tok % PLAN_ROWS == 0
    return pl.pallas_call(
        functools.partial(_moe_plan_kernel, n_experts=n_experts, n_tok=n_tok),
        out_shape=[jax.ShapeDtypeStruct((SUBLANES, n_tok), I32), jax.ShapeDtypeStruct((n_tok, LANES), F32),
                   jax.ShapeDtypeStruct((SUBLANES, LANES), I32)],
        compiler_params=pltpu.CompilerParams(vmem_limit_bytes=VMEM_LIMIT_BYTES),
        name="moe_plan",
    )(sel, comb)


def _row_copy(src_ref, src_row, dst_ref, dst_row, sem):
    return pltpu.make_async_copy(src_ref.at[pl.ds(src_row, 1), :], dst_ref.at[pl.ds(dst_row, 1), :], sem)


def _moe_scatter_kernel(dest_ref, x_ref, init_ref, xs_ref, sem, *, tb):
    del init_ref

    def start(n, carry):
        for k in range(TOP_K):
            _row_copy(x_ref, n, xs_ref, dest_ref[k, n], sem).start()
        return carry

    def wait(n, carry):
        for k in range(TOP_K):
            _row_copy(x_ref, n, xs_ref, dest_ref[k, n], sem).wait()
        return carry

    lax.fori_loop(0, tb, start, 0)
    lax.fori_loop(0, tb, wait, 0)


def moe_scatter(dest, xn, rows):
    n_tok, d = xn.shape
    tb = _pick_tile(n_tok, 256)
    any_spec = pl.BlockSpec(memory_space=pl.ANY)
    return pl.pallas_call(
        functools.partial(_moe_scatter_kernel, tb=tb),
        grid=(n_tok // tb,),
        in_specs=[pl.BlockSpec((SUBLANES, tb), lambda i: (0, i), memory_space=pltpu.SMEM),
                  pl.BlockSpec((tb, d), lambda i: (i, 0)), any_spec],
        out_specs=any_spec,
        out_shape=jax.ShapeDtypeStruct((rows, d), xn.dtype),
        scratch_shapes=[pltpu.SemaphoreType.DMA(())],
        input_output_aliases={2: 0},
        compiler_params=_params("arbitrary"),
        name="moe_scatter",
    )(dest, xn, jnp.zeros((rows, d), xn.dtype))


def _moe_up_kernel(te_ref, tv_ref, a_ref, wg_ref, wu_ref, o_ref, abf_ref):
    i, j = pl.program_id(0), pl.program_id(1)

    @pl.when(j == 0)
    def _():
        abf_ref[...] = a_ref[...].astype(BF16)

    @pl.when(tv_ref[i] == 0)
    def _():
        o_ref[...] = jnp.zeros_like(o_ref)

    @pl.when(tv_ref[i] != 0)
    def _():
        a = abf_ref[...]
        g = jnp.dot(a, wg_ref[...], preferred_element_type=F32)
        u = jnp.dot(a, wu_ref[...], preferred_element_type=F32)
        o_ref[...] = ((g * (1.0 / (1.0 + jnp.exp(-g)))) * u).astype(o_ref.dtype)


def moe_up(tile_expert, tile_valid, xs, wg, wu, tn=512):
    rows, kdim = xs.shape
    f = wg.shape[-1]
    tm, tn = MOE_ROW_TILE, _pick_tile(f, tn)
    w_spec = pl.BlockSpec((None, kdim, tn), lambda i, j, te, tv: (te[i], 0, j))
    return pl.pallas_call(
        _moe_up_kernel,
        grid_spec=pltpu.PrefetchScalarGridSpec(
            num_scalar_prefetch=2,
            grid=(rows // tm, f // tn),
            in_specs=[pl.BlockSpec((tm, kdim), lambda i, j, te, tv: (i, 0)), w_spec, w_spec],
            out_specs=pl.BlockSpec((tm, tn), lambda i, j, te, tv: (i, j)),
            scratch_shapes=[pltpu.VMEM((tm, kdim), BF16)]),
        out_shape=jax.ShapeDtypeStruct((rows, f), BF16),
        compiler_params=_params("arbitrary", "arbitrary"),
        name="moe_up",
    )(tile_expert, tile_valid, xs, wg, wu)


def _moe_down_kernel(te_ref, tv_ref, a_ref, w_ref, o_ref):
    i = pl.program_id(0)

    @pl.when(tv_ref[i] == 0)
    def _():
        o_ref[...] = jnp.zeros_like(o_ref)

    @pl.when(tv_ref[i] != 0)
    def _():
        o_ref[...] = jnp.dot(a_ref[...], w_ref[...], preferred_element_type=F32)


def moe_down(tile_expert, tile_valid, up, wd, tn=512):
    rows, f = up.shape
    d = wd.shape[-1]
    tm, tn = MOE_ROW_TILE, _pick_tile(d, tn)
    return pl.pallas_call(
        _moe_down_kernel,
        grid_spec=pltpu.PrefetchScalarGridSpec(
            num_scalar_prefetch=2,
            grid=(rows // tm, d // tn),
            in_specs=[pl.BlockSpec((tm, f), lambda i, j, te, tv: (i, 0)),
                      pl.BlockSpec((None, f, tn), lambda i, j, te, tv: (te[i], 0, j))],
            out_specs=pl.BlockSpec((tm, tn), lambda i, j, te, tv: (i, j))),
        out_shape=jax.ShapeDtypeStruct((rows, d), F32),
        compiler_params=_params("parallel", "parallel"),
        name="moe_down",
    )(tile_expert, tile_valid, up, wd)


def _moe_combine_kernel(dest_ref, gate_ref, h_ref, y_ref, o_ref, ylo_ref, yhi_ref, sem, *, tb):
    bufs = (ylo_ref, yhi_ref)

    def start(n, carry):
        for k in range(TOP_K):
            _row_copy(y_ref, dest_ref[k, n], bufs[k], n, sem).start()
        return carry

    def wait(n, carry):
        for k in range(TOP_K):
            _row_copy(y_ref, dest_ref[k, n], bufs[k], n, sem).wait()
        return carry

    lax.fori_loop(0, tb, start, 0)
    lax.fori_loop(0, tb, wait, 0)
    g = gate_ref[...]
    o_ref[...] = h_ref[...] + (g[:, 0:1] * ylo_ref[...] + g[:, 1:2] * yhi_ref[...])


def moe_combine(dest, gates, h, y):
    n_tok, d = h.shape
    tb = _pick_tile(n_tok, 256)
    return pl.pallas_call(
        functools.partial(_moe_combine_kernel, tb=tb),
        grid=(n_tok // tb,),
        in_specs=[pl.BlockSpec((SUBLANES, tb), lambda i: (0, i), memory_space=pltpu.SMEM),
                  pl.BlockSpec((tb, LANES), lambda i: (i, 0)),
                  pl.BlockSpec((tb, d), lambda i: (i, 0)),
                  pl.BlockSpec(memory_space=pl.ANY)],
        out_specs=pl.BlockSpec((tb, d), lambda i: (i, 0)),
        out_shape=jax.ShapeDtypeStruct((n_tok, d), F32),
        scratch_shapes=[pltpu.VMEM((tb, d), F32), pltpu.VMEM((tb, d), F32), pltpu.SemaphoreType.DMA(())],
        compiler_params=_params("arbitrary"),
        name="moe_combine",
    )(dest, gates, h, y)


def _dsa_swa_layer(h, hn, w_in, kv_norm, w_uk, w_uv, sinks, w_out, rel_bias, bsz, s_len):
    n_tok, d = h.shape
    a_heads = w_uk.shape[0]
    b_heads = sinks.shape[0]
    qa_w, qb_w = a_heads * HEAD_DIM, b_heads * HEAD_DIM
    kv_w = B_KV_HEADS * HEAD_DIM
    qi_w = IDX_HEADS * IDX_DIM
    widths = (qa_w, A_LATENT, qi_w, IDX_DIM, IDX_HEADS, qb_w, kv_w, kv_w)
    off = np.concatenate([[0], np.cumsum(widths)])
    seg = lambda s: w_in[:, int(off[s]):int(off[s + 1])]
    tail_pad = LANES - IDX_DIM - IDX_HEADS
    w_perm = jnp.concatenate([seg(0), seg(5), seg(6), seg(7), seg(1), seg(2), seg(3), seg(4),
                              jnp.zeros((d, tail_pad), w_in.dtype)], axis=1).astype(BF16)
    assert qa_w == qb_w and (qa_w + qb_w) % (2 * kv_w) == 0
    assert (qa_w + qb_w + 2 * kv_w) % A_LATENT == 0 and (qa_w + qb_w + 2 * kv_w + A_LATENT) % qi_w == 0
    kvb_block = (qa_w + qb_w) // (2 * kv_w)
    ckv_block = (qa_w + qb_w + 2 * kv_w) // A_LATENT
    qi_block = (qa_w + qb_w + 2 * kv_w + A_LATENT) // qi_w
    kw_block = (qa_w + qb_w + 2 * kv_w + A_LATENT + qi_w) // LANES

    proj = matmul(hn, w_perm, out_dtype=F32, tn=896)
    proj3 = proj.reshape(bsz, s_len, -1)
    topk = min(TOPK_MAX, s_len // 4)

    packed = latent_pack(proj, ckv_block, kv_norm, RMS_EPS)
    planes = packed.shape[0]
    ckv = jnp.transpose(packed, (1, 0, 2)).reshape(bsz, planes * s_len, LANES)
    idx = dsa_index(proj3, qi_block, kw_block, topk, row_scale=planes)
    q_lat = head_matmul(proj, w_uk.astype(BF16), out_dtype=BF16)
    bias_t = jnp.pad(rel_bias[:, :a_heads].T.astype(F32), ((0, 0), (0, LANES - NUM_BUCKETS)))
    o_lat = dsa_attention(idx, q_lat.reshape(bsz, s_len, a_heads, A_LATENT), ckv, bias_t)
    o_a = head_matmul(o_lat.reshape(n_tok, a_heads * A_LATENT), w_uv.astype(BF16), out_dtype=BF16)

    o_b = swa_attention(proj3, 1, kvb_block, rel_bias.astype(F32), sinks.astype(F32), a_heads)
    o = jnp.concatenate([o_a, o_b.reshape(n_tok, qb_w)], axis=1)
    return matmul(o, w_out.astype(BF16), out_dtype=F32, res=h)


def _diff_layer(h, hn, w_in, lam_vecs, subln, w_out, rel_bias, lambda_init, bsz, s_len):
    n_tok = h.shape[0]
    nheads = w_out.shape[0] // subln.shape[0]
    proj = matmul(hn, w_in.astype(BF16), out_dtype=BF16)
    o = diff_attention(proj.reshape(bsz, s_len, -1), rel_bias.astype(F32), lam_vecs.astype(F32),
                       subln.astype(F32), nheads, lambda_init)
    return matmul(o.reshape(n_tok, -1), w_out.astype(BF16), out_dtype=F32, res=h)


def _moe_layer(h, g, w_router, w_gate, w_up, w_down):
    n_tok, n_experts = h.shape[0], w_router.shape[1]
    xn, comb, sel = norm_router(h, g, w_router, RMS_EPS)
    dest, gates, tiles = moe_plan(sel, comb, n_experts)
    assert (TOP_K * n_tok) % MOE_ROW_TILE == 0
    rows = TOP_K * n_tok + n_experts * MOE_ROW_TILE
    assert rows // MOE_ROW_TILE <= LANES
    xs = moe_scatter(dest, xn, rows)
    up = moe_up(tiles[0], tiles[1], xs, w_gate.astype(BF16), w_up.astype(BF16))
    y = moe_down(tiles[0], tiles[1], up, w_down.astype(BF16))
    return moe_combine(dest, gates, h, y)


def kernel(x, ln_mix, ln_ffn, rel_bias, w_in_ab, kv_norm, w_uk, w_uv, sinks, w_out_ab, w_gate_d, w_up_d,
           w_down_d, w_in_c, lambda_q1, lambda_k1, lambda_q2, lambda_k2, subln, w_out_c, w_router,
           w_gate_e, w_up_e, w_down_e, ln_final):
    bsz, s_len, d = x.shape
    h = x.reshape(bsz * s_len, d)
    for i in range(ln_mix.shape[0]):
        j = i // 2
        hn = rmsnorm(h, ln_mix[i], RMS_EPS, BF16)
        if i % 2 == 0:
            h = _dsa_swa_layer(h, hn, w_in_ab[j], kv_norm[j], w_uk[j], w_uv[j], sinks[j], w_out_ab[j],
                               rel_bias, bsz, s_len)
            hf = rmsnorm(h, ln_ffn[i], RMS_EPS, BF16)
            up = swiglu_up(hf, w_gate_d[j].astype(BF16), w_up_d[j].astype(BF16))
            h = matmul(up, w_down_d[j].astype(BF16), out_dtype=F32, res=h)
        else:
            lambda_init = 0.8 - 0.6 * math.exp(-0.3 * i)
            lam_vecs = jnp.stack([lambda_q1[j], lambda_k1[j], lambda_q2[j], lambda_k2[j]])
            h = _diff_layer(h, hn, w_in_c[j], lam_vecs, subln[j], w_out_c[j], rel_bias, lambda_init,
                            bsz, s_len)
            h = _moe_layer(h, ln_ffn[i], w_router[j], w_gate_e[j], w_up_e[j], w_down_e[j])
    return rmsnorm(h, ln_final, RMS_EPS, x.dtype).reshape(bsz, s_len, d)
```

```python
import functools
import math

import numpy as np
import jax
import jax.numpy as jnp
from jax import lax
from jax.experimental import pallas as pl
from jax.experimental.pallas import tpu as pltpu

F32 = jnp.float32
BF16 = jnp.bfloat16
I32 = jnp.int32

HEAD_DIM = 128
A_LATENT = 512
IDX_HEADS = 16
IDX_DIM = 64
TOPK_MAX = 256
B_KV_HEADS = 2
WINDOW = 128
NUM_BUCKETS = 32
MAX_DISTANCE = 128
TOP_K = 2
RMS_EPS = 1e-6
SUBLN_EPS = 1e-5

LANES = 128
SUBLANES = 8
VMEM_LIMIT_BYTES = 56 * 1024 * 1024
INT_MIN = -(2 ** 31)
NEG_INF = float("-inf")

GATHER_STRIDE = TOPK_MAX + SUBLANES
LATENT_PLANES = A_LATENT // (2 * LANES)
RANKS_PER_STEP = 4
CHUNK_GROUP = 4


def _bucket_thresholds():
    n = np.arange(0, 4 * MAX_DISTANCE, dtype=np.int64)
    max_exact = NUM_BUCKETS // 2
    nf = np.maximum(n, 1).astype(np.float32)
    large = max_exact + (
        np.log(nf / np.float32(max_exact)) / np.float32(math.log(MAX_DISTANCE / max_exact))
        * np.float32(NUM_BUCKETS - max_exact)
    ).astype(np.int32)
    large = np.minimum(large, NUM_BUCKETS - 1)
    bucket = np.where(n < max_exact, n, large)
    assert np.all(np.diff(bucket) >= 0) and bucket[-1] == NUM_BUCKETS - 1
    return tuple(int(np.argmax(bucket >= k)) for k in range(1, NUM_BUCKETS))


BUCKET_THRESHOLDS = _bucket_thresholds()
FAR_DISTANCE = BUCKET_THRESHOLDS[-1]


def _bias_from_distance(dist, table_entry):
    val = jnp.where(dist >= BUCKET_THRESHOLDS[0], table_entry(1), table_entry(0))
    for k in range(2, NUM_BUCKETS):
        val = jnp.where(dist >= BUCKET_THRESHOLDS[k - 1], table_entry(k), val)
    return val


def _pick_tile(dim, target, align=LANES):
    if dim <= target:
        return dim
    t = (target // align) * align
    while t >= align:
        if dim % t == 0:
            return t
        t -= align
    return dim


def _params(*semantics):
    return pltpu.CompilerParams(dimension_semantics=semantics, vmem_limit_bytes=VMEM_LIMIT_BYTES)


def _rmsnorm_kernel(x_ref, g_ref, o_ref, *, eps):
    x = x_ref[...].astype(F32)
    y = x * lax.rsqrt(jnp.mean(x * x, axis=-1, keepdims=True) + eps)
    o_ref[...] = (y * g_ref[...].astype(F32)).astype(o_ref.dtype)


def rmsnorm(x2d, g, eps, out_dtype):
    m, d = x2d.shape
    tm = _pick_tile(m, 256, SUBLANES)
    return pl.pallas_call(
        functools.partial(_rmsnorm_kernel, eps=eps),
        grid=(m // tm,),
        in_specs=[pl.BlockSpec((tm, d), lambda i: (i, 0)), pl.BlockSpec((1, d), lambda i: (0, 0))],
        out_specs=pl.BlockSpec((tm, d), lambda i: (i, 0)),
        out_shape=jax.ShapeDtypeStruct((m, d), out_dtype),
        compiler_params=_params("parallel"),
        name="rmsnorm",
    )(x2d, g.reshape(1, d))


def _matmul_kernel(*refs, nk, has_res, scale_col):
    a_ref, w_ref = refs[0], refs[1]
    pos = 2
    res_ref = scale_ref = None
    if has_res:
        res_ref = refs[pos]
        pos += 1
    if scale_col is not None:
        scale_ref = refs[pos]
        pos += 1
    o_ref = refs[pos]
    k = pl.program_id(2)

    def finish(y):
        if scale_ref is not None:
            y = y * scale_ref[:, scale_col:scale_col + 1]
        if res_ref is not None:
            y = res_ref[...].astype(F32) + y
        o_ref[...] = y.astype(o_ref.dtype)

    part = jnp.dot(a_ref[...], w_ref[...], preferred_element_type=F32)
    if nk == 1:
        finish(part)
        return
    acc_ref = refs[pos + 1]

    @pl.when(k == 0)
    def _():
        acc_ref[...] = part

    @pl.when(k > 0)
    def _():
        acc_ref[...] += part

    @pl.when(k == nk - 1)
    def _():
        finish(acc_ref[...])


def matmul(a, w, *, out_dtype, res=None, row_scale=None, scale_col=None, expert=None,
           tm=1024, tn=512, tk=4096):
    m, kdim = a.shape
    n = w.shape[-1]
    tm, tn, tk = _pick_tile(m, tm, SUBLANES), _pick_tile(n, tn), _pick_tile(kdim, tk)
    nk = kdim // tk
    if expert is None:
        w_spec = pl.BlockSpec((tk, tn), lambda i, j, k: (k, j))
    else:
        w_spec = pl.BlockSpec((None, tk, tn), lambda i, j, k: (expert, k, j))
    in_specs = [pl.BlockSpec((tm, tk), lambda i, j, k: (i, k)), w_spec]
    args = [a, w]
    if res is not None:
        in_specs.append(pl.BlockSpec((tm, tn), lambda i, j, k: (i, j)))
        args.append(res)
    if row_scale is not None:
        in_specs.append(pl.BlockSpec((tm, row_scale.shape[1]), lambda i, j, k: (i, 0)))
        args.append(row_scale)
    return pl.pallas_call(
        functools.partial(_matmul_kernel, nk=nk, has_res=res is not None,
                          scale_col=scale_col if row_scale is not None else None),
        grid=(m // tm, n // tn, nk),
        in_specs=in_specs,
        out_specs=pl.BlockSpec((tm, tn), lambda i, j, k: (i, j)),
        out_shape=jax.ShapeDtypeStruct((m, n), out_dtype),
        scratch_shapes=[pltpu.VMEM((tm, tn), F32)] if nk > 1 else [],
        compiler_params=_params("parallel", "parallel", "arbitrary"),
        name="matmul",
    )(*args)


def _swiglu_up_kernel(a_ref, wg_ref, wu_ref, o_ref, *acc_refs, nk):
    k = pl.program_id(2)

    def finish(g, u):
        silu = g * (1.0 / (1.0 + jnp.exp(-g)))
        o_ref[...] = (silu * u).astype(o_ref.dtype)

    a = a_ref[...]
    gate = jnp.dot(a, wg_ref[...], preferred_element_type=F32)
    up = jnp.dot(a, wu_ref[...], preferred_element_type=F32)
    if nk == 1:
        finish(gate, up)
        return
    accg_ref, accu_ref = acc_refs

    @pl.when(k == 0)
    def _():
        accg_ref[...] = gate
        accu_ref[...] = up

    @pl.when(k > 0)
    def _():
        accg_ref[...] += gate
        accu_ref[...] += up

    @pl.when(k == nk - 1)
    def _():
        finish(accg_ref[...], accu_ref[...])


def swiglu_up(a, wg, wu, *, expert=None, tm=1024, tn=512, tk=4096):
    m, kdim = a.shape
    f = wg.shape[-1]
    tm, tn, tk = _pick_tile(m, tm, SUBLANES), _pick_tile(f, tn), _pick_tile(kdim, tk)
    nk = kdim // tk
    if expert is None:
        w_spec = pl.BlockSpec((tk, tn), lambda i, j, k: (k, j))
    else:
        w_spec = pl.BlockSpec((None, tk, tn), lambda i, j, k: (expert, k, j))
    return pl.pallas_call(
        functools.partial(_swiglu_up_kernel, nk=nk),
        grid=(m // tm, f // tn, nk),
        in_specs=[pl.BlockSpec((tm, tk), lambda i, j, k: (i, k)), w_spec, w_spec],
        out_specs=pl.BlockSpec((tm, tn), lambda i, j, k: (i, j)),
        out_shape=jax.ShapeDtypeStruct((m, f), BF16),
        scratch_shapes=[pltpu.VMEM((tm, tn), F32), pltpu.VMEM((tm, tn), F32)] if nk > 1 else [],
        compiler_params=_params("parallel", "parallel", "arbitrary"),
        name="swiglu_up",
    )(a, wg, wu)


def _head_matmul_kernel(a_ref, w_ref, o_ref):
    o_ref[...] = jnp.dot(a_ref[...].astype(BF16), w_ref[...], preferred_element_type=F32).astype(o_ref.dtype)


def head_matmul(a, w3, *, out_dtype, tm=1024):
    m = a.shape[0]
    nheads, ka, nb = w3.shape
    tm = _pick_tile(m, tm, SUBLANES)
    return pl.pallas_call(
        _head_matmul_kernel,
        grid=(m // tm, nheads),
        in_specs=[pl.BlockSpec((tm, ka), lambda i, h: (i, h)),
                  pl.BlockSpec((None, ka, nb), lambda i, h: (h, 0, 0))],
        out_specs=pl.BlockSpec((tm, nb), lambda i, h: (i, h)),
        out_shape=jax.ShapeDtypeStruct((m, nheads * nb), out_dtype),
        compiler_params=_params("parallel", "parallel"),
        name="head_matmul",
    )(a, w3)


def _bits(x):
    return lax.bitcast_convert_type(x, I32)


def _latent_pack_kernel(c_ref, g_ref, o_ref, *, eps):
    x = c_ref[...]
    y = x * lax.rsqrt(jnp.mean(x * x, axis=-1, keepdims=True) + eps) * g_ref[...]
    yb = _bits(y.astype(BF16).astype(F32))
    for r in range(A_LATENT // (2 * LANES)):
        lo = lax.shift_right_logical(yb[:, (2 * r) * LANES:(2 * r + 1) * LANES], 16)
        hi = yb[:, (2 * r + 1) * LANES:(2 * r + 2) * LANES] & jnp.int32(-65536)
        o_ref[r] = lo | hi


def latent_pack(proj, col_block, g, eps):
    m = proj.shape[0]
    tm = _pick_tile(m, 512, SUBLANES)
    planes = A_LATENT // (2 * LANES)
    return pl.pallas_call(
        functools.partial(_latent_pack_kernel, eps=eps),
        grid=(m // tm,),
        in_specs=[pl.BlockSpec((tm, A_LATENT), lambda i: (i, col_block)),
                  pl.BlockSpec((1, A_LATENT), lambda i: (0, 0))],
        out_specs=pl.BlockSpec((planes, tm, LANES), lambda i: (0, i, 0)),
        out_shape=jax.ShapeDtypeStruct((planes, m, LANES), I32),
        compiler_params=_params("parallel"),
        name="latent_pack",
    )(proj, g.reshape(1, A_LATENT))


def _dsa_index_kernel(qi_ref, kw_ref, kwq_ref, idx_ref, qt_ref, key_ref, rank_ref, acc_ref, run_ref, bnd_ref,
                      *, tq, topk, row_scale):
    ch = tq
    i = pl.program_id(1)
    nchunks = i + 1

    q_t = qi_ref[0].T
    for h in range(IDX_HEADS):
        qt_ref[:, h * tq:(h + 1) * tq] = q_t[h * IDX_DIM:(h + 1) * IDX_DIM, :].astype(BF16)
    w_t = kwq_ref[0].T[IDX_DIM:IDX_DIM + IDX_HEADS, :] * (IDX_DIM ** -0.5 * IDX_HEADS ** -0.5)
    t_row = i * tq + lax.broadcasted_iota(I32, (1, tq), 1)
    s_iota = lax.broadcasted_iota(I32, (ch, tq), 0)

    def chunk_start(c):
        return pl.multiple_of(c * ch, ch)

    group = CHUNK_GROUP
    ngroups = (nchunks + (group - 1)) >> (group.bit_length() - 1)

    def score_group(g, carry):
        rows = group * ch
        g0 = pl.multiple_of(g * rows, rows)
        kc = kw_ref[0, pl.ds(g0, rows), :][:, :IDX_DIM].astype(BF16)
        dots = jnp.dot(kc, qt_ref[...], preferred_element_type=F32)
        for u in range(group):
            sc = jnp.zeros((ch, tq), F32)
            for h in range(IDX_HEADS):
                sc = sc + jnp.maximum(dots[u * ch:(u + 1) * ch, h * tq:(h + 1) * tq], 0.0) * w_t[h:h + 1, :]
            bits = _bits(sc)
            key = bits ^ ((bits >> 31) & jnp.int32(0x7FFFFFFF))
            r0 = g0 + u * ch
            key_ref[pl.ds(r0, ch), :] = jnp.where(r0 + s_iota <= t_row, key, jnp.int32(INT_MIN))
        return carry

    lax.fori_loop(0, ngroups, score_group, 0)

    def column_sum(m):
        return m.reshape(m.shape[0] // SUBLANES, SUBLANES, tq).sum(axis=0)

    def count(preds):
        rows = group * ch

        def body(g, parts):
            k = key_ref[pl.ds(pl.multiple_of(g * rows, rows), rows), :]
            return tuple(part + column_sum(jnp.where(pred(k), 1, 0).astype(I32)) for part, pred in zip(parts, preds))
        parts = lax.fori_loop(0, ngroups, body, tuple(jnp.zeros((SUBLANES, tq), I32) for _ in preds))
        return tuple(part.sum(axis=0, keepdims=True) for part in parts)

    def at_least(cand):
        return lambda k: k >= cand

    thr = jnp.where(count((at_least(0),))[0] >= topk, 0, INT_MIN).astype(I32)

    def next_bit(b, thr):
        cand = thr | lax.shift_left(jnp.int32(1), 30 - b)
        return jnp.where(count((at_least(cand),))[0] >= topk, cand, thr)

    thr = lax.fori_loop(0, 31, next_bit, thr)
    n_gt, n_ge = count((lambda k: k > thr, lambda k: (k >= thr) & (k > INT_MIN)))
    need = (topk - n_gt).astype(F32)
    has_ties = jnp.max(n_ge) > topk

    tri = (lax.broadcasted_iota(I32, (ch, ch), 0) >= lax.broadcasted_iota(I32, (ch, ch), 1)).astype(BF16)

    def select_chunk(with_ties, c, carry):
        run, eq_run = carry
        r0 = chunk_start(c)
        k = key_ref[pl.ds(r0, ch), :]
        if with_ties:
            eq = jnp.where(k == thr, jnp.where(k > INT_MIN, 1.0, 0.0), 0.0)
            eq_cum = jnp.dot(tri, eq.astype(BF16), preferred_element_type=F32) + eq_run
            sel = jnp.where(k > thr, 1.0, jnp.where(eq_cum <= need, eq, 0.0))
            eq_run = eq_cum[ch - 1:ch, :]
        else:
            sel = jnp.where(k >= thr, jnp.where(k > INT_MIN, 1.0, 0.0), 0.0)
        rank_ref[pl.ds(r0, ch), :] = sel * jnp.dot(tri, sel.astype(BF16), preferred_element_type=F32)
        n_sel = column_sum(sel).sum(axis=0, keepdims=True)
        run_ref[c] = run
        bnd_ref[0, c] = jnp.min(run).astype(I32)
        bnd_ref[1, c] = jnp.max(run).astype(I32)
        bnd_ref[2, c] = jnp.max(n_sel).astype(I32)
        return run + n_sel, eq_run

    zero_row = jnp.zeros((1, tq), F32)

    def select_group(with_ties, g, carry):
        for u in range(group):
            carry = select_chunk(with_ties, g * group + u, carry)
        return carry

    @pl.when(has_ties)
    def _():
        lax.fori_loop(0, ngroups, functools.partial(select_group, True), (zero_row, zero_row))

    @pl.when(jnp.logical_not(has_ties))
    def _():
        lax.fori_loop(0, ngroups, functools.partial(select_group, False), (zero_row, zero_row))

    acc_ref[...] = jnp.zeros_like(acc_ref)
    sub_iota = lax.broadcasted_iota(I32, (SUBLANES, tq), 0)
    last_tile = topk // SUBLANES - 1

    def compact_chunk(c, carry):
        r0 = chunk_start(c)
        run_lo, run_hi = bnd_ref[0, c], bnd_ref[1, c]
        run = run_ref[c]
        key_index_1 = (r0 + 1 + s_iota).astype(F32)

        def place_ranks(g, carry):
            rank = rank_ref[pl.ds(r0, ch), :]
            first = g * RANKS_PER_STEP
            slots, keys = [], []
            for u in range(RANKS_PER_STEP):
                r = (first + u).astype(F32)
                hit = jnp.where(rank == r + 1.0, key_index_1, 0.0)
                found = column_sum(hit).sum(axis=0, keepdims=True)
                slots.append(jnp.where(found > 0.0, run + r, -1.0))
                keys.append(found - 1.0)

            def place_tile(t8, carry):
                j0 = pl.multiple_of(t8 * SUBLANES, SUBLANES)
                rows = (j0 + sub_iota).astype(F32)
                cur = acc_ref[pl.ds(j0, SUBLANES), :]
                for slot, key in zip(slots, keys):
                    cur = jnp.where(rows == slot, key, cur)
                acc_ref[pl.ds(j0, SUBLANES), :] = cur
                return carry

            hi_tile = jnp.minimum((run_hi + first + (RANKS_PER_STEP - 1)) >> 3, last_tile)
            lax.fori_loop((run_lo + first) >> 3, hi_tile + 1, place_tile, 0)
            return carry

        steps = (bnd_ref[2, c] + (RANKS_PER_STEP - 1)) >> (RANKS_PER_STEP.bit_length() - 1)
        lax.fori_loop(0, steps, place_ranks, 0)
        return carry

    lax.fori_loop(0, nchunks, compact_chunk, 0)
    idx_ref[0] = (acc_ref[...] * float(row_scale)).T.astype(I32)


def dsa_index(proj3, qi_block, kw_block, topk, row_scale=1, tq=128):
    bsz, s_len, _ = proj3.shape
    tq = min(tq, s_len)
    nq = s_len // tq
    assert nq % CHUNK_GROUP == 0
    return pl.pallas_call(
        functools.partial(_dsa_index_kernel, tq=tq, topk=topk, row_scale=row_scale),
        grid=(bsz, nq),
        in_specs=[pl.BlockSpec((1, tq, IDX_HEADS * IDX_DIM), lambda b, i: (b, i, qi_block)),
                  pl.BlockSpec((1, s_len, LANES), lambda b, i: (b, 0, kw_block)),
                  pl.BlockSpec((1, tq, LANES), lambda b, i: (b, i, kw_block))],
        out_specs=pl.BlockSpec((1, tq, topk), lambda b, i: (b, i, 0)),
        out_shape=jax.ShapeDtypeStruct((bsz, s_len, topk), I32),
        scratch_shapes=[pltpu.VMEM((IDX_DIM, IDX_HEADS * tq), BF16),
                        pltpu.VMEM((s_len, tq), I32),
                        pltpu.VMEM((s_len, tq), F32),
                        pltpu.VMEM((topk, tq), F32),
                        pltpu.VMEM((nq, 1, tq), F32),
                        pltpu.SMEM((3, nq), I32)],
        compiler_params=_params("parallel", "arbitrary"),
        name="dsa_index",
    )(proj3, proj3, proj3)


def _dsa_attn_kernel(idx_s_ref, idx_v_ref, ql_ref, ckv_ref, tab_ref, o_ref, *tile_refs, tq, topk, nheads):
    i = pl.program_id(1)
    planes = LATENT_PLANES
    plane_shift = planes.bit_length() - 1
    assert planes == 1 << plane_shift
    lane_iota = lax.broadcasted_iota(I32, (nheads, topk), 1)
    scale = HEAD_DIM ** -0.5

    def gather(q, tile_ref, lo, hi):
        for mi in range(lo, hi):
            row = pl.multiple_of(idx_s_ref[0, q, mi], planes)
            tile_ref[pl.ds(mi, planes, stride=GATHER_STRIDE), :] = ckv_ref[0, pl.ds(row, planes), :]

    def attend(queries, tiles):
        nq = len(queries)
        xs = [[] for _ in queries]
        for r in range(planes):
            for n in range(nq):
                w = tiles[n][r * GATHER_STRIDE:r * GATHER_STRIDE + topk, :]
                xs[n].append(lax.bitcast_convert_type(lax.shift_left(w, 16), F32).astype(BF16))
                xs[n].append(lax.bitcast_convert_type(w & jnp.int32(-65536), F32).astype(BF16))
            yield
        logits = []
        for n, q in enumerate(queries):
            ql = ql_ref[0, q]
            acc = jnp.zeros((nheads, topk), F32)
            for j, x in enumerate(xs[n]):
                acc = acc + lax.dot_general(ql[:, j * LANES:(j + 1) * LANES], x, (((1,), (1,)), ((), ())),
                                            preferred_element_type=F32)
            logits.append(acc)
        yield
        bias = []
        for q in queries:
            dist = jnp.broadcast_to(i * tq + q - (idx_v_ref[0, q] >> plane_shift), (nheads, topk))
            bias.append(_bias_from_distance(dist, lambda k: tab_ref[:, k:k + 1]))
        yield
        es, inv = [], []
        for n, q in enumerate(queries):
            z = jnp.where(lane_iota < jnp.minimum(i * tq + q + 1, topk), logits[n] * scale + bias[n], NEG_INF)
            e = jnp.exp(z - jnp.max(z, axis=-1, keepdims=True))
            inv.append(1.0 / jnp.sum(e, axis=-1, keepdims=True))
            es.append(e.astype(BF16))
        yield
        for j in range(2 * planes):
            for n, q in enumerate(queries):
                o = jnp.dot(es[n], xs[n][j], preferred_element_type=F32) * inv[n]
                o_ref[0, q, :, j * LANES:(j + 1) * LANES] = o.astype(o_ref.dtype)
            if j % 2 == 1:
                yield

    n_stages = 2 * planes + 3
    rows_per_stage = topk // n_stages

    def attend_while_gathering(queries, tiles, next_queries, next_tiles):
        stages = attend(queries, tiles)
        for s in range(n_stages):
            next(stages)
            hi = topk if s == n_stages - 1 else (s + 1) * rows_per_stage
            for q, tile_ref in zip(next_queries, next_tiles):
                gather(q, tile_ref, s * rows_per_stage, hi)

    half = len(tile_refs) // 2
    first, second = tile_refs[:half], tile_refs[half:]
    for n in range(half):
        gather(n, first[n], 0, topk)

    def query_group(qg, carry):
        q0 = 2 * half * qg
        qa = [q0 + n for n in range(half)]
        qb = [q0 + half + n for n in range(half)]
        qc = [jnp.minimum(q0 + 2 * half + n, tq - 1) for n in range(half)]
        attend_while_gathering(qa, first, qb, second)
        attend_while_gathering(qb, second, qc, first)
        return carry

    lax.fori_loop(0, tq // (2 * half), query_group, 0)


def dsa_attention(idx, q_lat, ckv_packed, bias_t, tq=64):
    bsz, s_len, topk = idx.shape
    nheads = q_lat.shape[2]
    tq = min(tq, s_len)
    planes = LATENT_PLANES
    return pl.pallas_call(
        functools.partial(_dsa_attn_kernel, tq=tq, topk=topk, nheads=nheads),
        grid=(bsz, s_len // tq),
        in_specs=[pl.BlockSpec((1, tq, topk), lambda b, i: (b, i, 0), memory_space=pltpu.SMEM),
                  pl.BlockSpec((1, tq, 1, topk), lambda b, i: (b, i, 0, 0)),
                  pl.BlockSpec((1, tq, nheads, A_LATENT), lambda b, i: (b, i, 0, 0)),
                  pl.BlockSpec((1, planes * s_len, LANES), lambda b, i: (b, 0, 0)),
                  pl.BlockSpec((nheads, LANES), lambda b, i: (0, 0))],
        out_specs=pl.BlockSpec((1, tq, nheads, A_LATENT), lambda b, i: (b, i, 0, 0)),
        out_shape=jax.ShapeDtypeStruct((bsz, s_len, nheads, A_LATENT), BF16),
        scratch_shapes=[pltpu.VMEM((planes * GATHER_STRIDE, LANES), I32)] * 8,
        compiler_params=_params("parallel", "arbitrary"),
        name="dsa_attention",
    )(idx, idx.reshape(bsz, s_len, 1, topk), q_lat, ckv_packed, bias_t)


def _swa_kernel(tab_ref, sink_ref, q_ref, kvp_ref, kvc_ref, o_ref, band_ref, *, nheads, slot0):
    w = WINDOW
    group = nheads // B_KV_HEADS
    b, n = pl.program_id(0), pl.program_id(1)
    rel = (lax.broadcasted_iota(I32, (w, 2 * w), 0) + w) - lax.broadcasted_iota(I32, (w, 2 * w), 1)

    @pl.when((b == 0) & (n == 0))
    def _():
        for hb in range(nheads):
            band_ref[hb] = _bias_from_distance(rel, lambda k: tab_ref[k, slot0 + hb])

    s_global = n * w - w + lax.broadcasted_iota(I32, (w, 2 * w), 1)
    mask = (rel >= 0) & (rel < w) & (s_global >= 0)
    scale = HEAD_DIM ** -0.5
    kv_width = B_KV_HEADS * HEAD_DIM
    for kh in range(B_KV_HEADS):
        ks = slice(kh * HEAD_DIM, (kh + 1) * HEAD_DIM)
        vs = slice(kv_width + kh * HEAD_DIM, kv_width + (kh + 1) * HEAD_DIM)
        kk = jnp.concatenate([kvp_ref[0][:, ks], kvc_ref[0][:, ks]], axis=0).astype(BF16)
        vv = jnp.concatenate([kvp_ref[0][:, vs], kvc_ref[0][:, vs]], axis=0).astype(BF16)
        for g in range(group):
            hb = kh * group + g
            cols = slice(hb * HEAD_DIM, (hb + 1) * HEAD_DIM)
            q = q_ref[0][:, cols].astype(BF16)
            z = lax.dot_general(q, kk, (((1,), (1,)), ((), ())), preferred_element_type=F32) * scale + band_ref[hb]
            z = jnp.where(mask, z, NEG_INF)
            sink = sink_ref[hb]
            m = jnp.maximum(jnp.max(z, axis=-1, keepdims=True), sink)
            e = jnp.exp(z - m)
            den = jnp.sum(e, axis=-1, keepdims=True) + jnp.exp(sink - m)
            o_ref[0, :, cols] = jnp.dot((e / den).astype(BF16), vv, preferred_element_type=F32).astype(o_ref.dtype)


def swa_attention(proj3, q_block, kv_block, rel_bias, sinks, slot0):
    bsz, s_len, _ = proj3.shape
    nheads = sinks.shape[0]
    qw = nheads * HEAD_DIM
    kvw = 2 * B_KV_HEADS * HEAD_DIM
    w = WINDOW
    smem = pl.BlockSpec(memory_space=pltpu.SMEM)
    return pl.pallas_call(
        functools.partial(_swa_kernel, nheads=nheads, slot0=slot0),
        grid=(bsz, s_len // w),
        in_specs=[smem, smem,
                  pl.BlockSpec((1, w, qw), lambda b, n: (b, n, q_block)),
                  pl.BlockSpec((1, w, kvw), lambda b, n: (b, jnp.maximum(n - 1, 0), kv_block)),
                  pl.BlockSpec((1, w, kvw), lambda b, n: (b, n, kv_block))],
        out_specs=pl.BlockSpec((1, w, qw), lambda b, n: (b, n, 0)),
        out_shape=jax.ShapeDtypeStruct((bsz, s_len, qw), BF16),
        scratch_shapes=[pltpu.VMEM((nheads, w, 2 * w), F32)],
        compiler_params=_params("arbitrary", "arbitrary"),
        name="swa_attention",
    )(rel_bias, sinks, proj3, proj3, proj3)


def _diff_attn_kernel(tab_ref, lam_ref, subln_ref, q1_ref, q2_ref, k1_ref, k2_ref, v_ref, o_ref,
                      band_ref, m_ref, l_ref, acc_ref, *, tq, nheads, lambda_init):
    h, qi = pl.program_id(1), pl.program_id(2)
    ii = lax.broadcasted_iota(I32, (tq, tq), 0)
    jj = lax.broadcasted_iota(I32, (tq, tq), 1)
    log2e = 1.4426950408889634
    qk_scale = HEAD_DIM ** -0.5 * log2e
    last = NUM_BUCKETS - 1
    ntile = tq // LANES
    vd = acc_ref.shape[-1]

    @pl.when(qi == 0)
    def _():
        for m in range(2):
            slot = m * nheads + h
            far = tab_ref[last, slot]
            band_ref[m, 0] = (_bias_from_distance(ii - jj, lambda k: tab_ref[k, slot]) - far) * log2e
            band_ref[m, 1] = (_bias_from_distance(ii - jj + tq, lambda k: tab_ref[k, slot]) - far) * log2e

    m_ref[...] = jnp.full_like(m_ref, NEG_INF)
    l_ref[...] = jnp.zeros_like(l_ref)
    acc_ref[...] = jnp.zeros_like(acc_ref)
    qs = (q1_ref[0], q2_ref[0])
    k_refs = (k1_ref, k2_ref)

    def block(kj, band_idx, causal):
        r0 = pl.multiple_of(kj * tq, tq)
        v = v_ref[0, pl.ds(r0, tq), :]
        scores = [lax.dot_general(qs[m], k_refs[m][0, pl.ds(r0, tq), :], (((1,), (1,)), ((), ())),
                                  preferred_element_type=F32) for m in range(2)]
        alphas, pvs = [], []
        for m in range(2):
            z = scores[m] * qk_scale
            if band_idx is not None:
                z = z + band_ref[m, band_idx]
            if causal:
                z = jnp.where(jj <= ii, z, NEG_INF)
            zt = [z[:, j * LANES:(j + 1) * LANES] for j in range(ntile)]
            m_old = m_ref[m]
            m_new = jnp.maximum(m_old, jnp.max(functools.reduce(jnp.maximum, zt), axis=-1, keepdims=True))
            alpha = jnp.exp2(m_old - m_new)
            es = [jnp.exp2(t - m_new) for t in zt]
            l_ref[m] = alpha * l_ref[m] + functools.reduce(jnp.add, es)
            m_ref[m] = m_new
            alphas.append(alpha)
            pvs.append(jnp.dot(jnp.concatenate([e.astype(BF16) for e in es], axis=1), v,
                               preferred_element_type=F32))
        for m in range(2):
            for c in range(vd // LANES):
                cols = slice(c * LANES, (c + 1) * LANES)
                acc_ref[m, :, cols] = alphas[m] * acc_ref[m, :, cols] + pvs[m][:, cols]

    def far_block(kj, carry):
        block(kj, None, causal=False)
        return carry

    lax.fori_loop(0, jnp.maximum(qi - 1, 0), far_block, 0)

    @pl.when(qi >= 1)
    def _():
        block(qi - 1, 1, causal=False)

    block(qi, 0, causal=True)

    lam = (jnp.exp(jnp.sum(lam_ref[0:1, :] * lam_ref[1:2, :], axis=-1, keepdims=True))
           - jnp.exp(jnp.sum(lam_ref[2:3, :] * lam_ref[3:4, :], axis=-1, keepdims=True)) + lambda_init)
    outs = [acc_ref[m] / jnp.sum(l_ref[m], axis=-1, keepdims=True) for m in range(2)]
    o = outs[0] - lam * outs[1]
    y = o * lax.rsqrt(jnp.mean(o * o, axis=-1, keepdims=True) + SUBLN_EPS)
    o_ref[0] = ((y * subln_ref[...]) * (1.0 - lambda_init)).astype(o_ref.dtype)


def diff_attention(proj3, rel_bias, lam_vecs, subln, nheads, lambda_init, tq=512):
    bsz, s_len, _ = proj3.shape
    tq = min(tq, s_len)
    assert tq > FAR_DISTANCE
    vd = 2 * HEAD_DIM
    smem = pl.BlockSpec(memory_space=pltpu.SMEM)
    q_spec = [pl.BlockSpec((1, tq, HEAD_DIM), functools.partial(lambda m, b, h, i: (b, i, m * nheads + h), m))
              for m in range(2)]
    k_spec = [pl.BlockSpec((1, s_len, HEAD_DIM),
                           functools.partial(lambda m, b, h, i: (b, 0, (2 + m) * nheads + h), m))
              for m in range(2)]
    v_spec = pl.BlockSpec((1, s_len, vd), lambda b, h, i: (b, 0, 2 * nheads + h))
    return pl.pallas_call(
        functools.partial(_diff_attn_kernel, tq=tq, nheads=nheads, lambda_init=lambda_init),
        grid=(bsz, nheads, s_len // tq),
        in_specs=[smem, pl.BlockSpec((4, HEAD_DIM), lambda b, h, i: (0, 0)),
                  pl.BlockSpec((1, vd), lambda b, h, i: (0, 0)),
                  q_spec[0], q_spec[1], k_spec[0], k_spec[1], v_spec],
        out_specs=pl.BlockSpec((1, tq, vd), lambda b, h, i: (b, i, h)),
        out_shape=jax.ShapeDtypeStruct((bsz, s_len, nheads * vd), BF16),
        scratch_shapes=[pltpu.VMEM((2, 2, tq, tq), F32), pltpu.VMEM((2, tq, LANES), F32),
                        pltpu.VMEM((2, tq, LANES), F32), pltpu.VMEM((2, tq, vd), F32)],
        compiler_params=_params("arbitrary", "arbitrary", "arbitrary"),
        name="diff_attention",
    )(rel_bias, lam_vecs, subln.reshape(1, vd), proj3, proj3, proj3, proj3, proj3)


def _norm_router_kernel(x_ref, g_ref, wr_ref, xn_ref, comb_ref, sel_ref, *, eps, n_experts):
    x = x_ref[...]
    y = x * lax.rsqrt(jnp.mean(x * x, axis=-1, keepdims=True) + eps) * g_ref[...]
    xn_ref[...] = y
    logits = jnp.dot(y, wr_ref[...], preferred_element_type=F32, precision=lax.Precision.HIGHEST)
    lane = lax.broadcasted_iota(I32, logits.shape, 1)
    z = jnp.where(lane < n_experts, logits, NEG_INF)
    v1 = jnp.max(z, axis=-1, keepdims=True)
    i1 = jnp.min(jnp.where(z == v1, lane, LANES), axis=-1, keepdims=True)
    z2 = jnp.where(lane == i1, NEG_INF, z)
    v2 = jnp.max(z2, axis=-1, keepdims=True)
    i2 = jnp.min(jnp.where(z2 == v2, lane, LANES), axis=-1, keepdims=True)
    e2 = jnp.exp(v2 - v1)
    den = 1.0 + e2
    comb_ref[...] = jnp.where(lane == i1, 1.0 / den, 0.0) + jnp.where(lane == i2, e2 / den, 0.0)
    sel_ref[...] = jnp.where(lane == i1, 1.0, 0.0) + jnp.where(lane == i2, 1.0, 0.0)


def norm_router(x2d, g, w_router, eps):
    m, d = x2d.shape
    n_experts = w_router.shape[1]
    tm = _pick_tile(m, 256, SUBLANES)
    wr = jnp.pad(w_router.astype(F32), ((0, 0), (0, LANES - n_experts)))
    lane_spec = pl.BlockSpec((tm, LANES), lambda i: (i, 0))
    return pl.pallas_call(
        functools.partial(_norm_router_kernel, eps=eps, n_experts=n_experts),
        grid=(m // tm,),
        in_specs=[pl.BlockSpec((tm, d), lambda i: (i, 0)), pl.BlockSpec((1, d), lambda i: (0, 0)),
                  pl.BlockSpec((d, LANES), lambda i: (0, 0))],
        out_specs=[pl.BlockSpec((tm, d), lambda i: (i, 0)), lane_spec, lane_spec],
        out_shape=[jax.ShapeDtypeStruct((m, d), F32), jax.ShapeDtypeStruct((m, LANES), F32),
                   jax.ShapeDtypeStruct((m, LANES), F32)],
        compiler_params=_params("parallel"),
        name="norm_router",
    )(x2d, g.reshape(1, d), wr)


MOE_ROW_TILE = 512
PLAN_ROWS = 256


def _moe_plan_kernel(sel_ref, comb_ref, dest_ref, gate_ref, tile_ref, *, n_experts, n_tok):
    rb = PLAN_ROWS
    nblk = n_tok // rb
    shift = MOE_ROW_TILE.bit_length() - 1
    lane = lax.broadcasted_iota(I32, (1, LANES), 1)

    def count_block(b, part):
        m = sel_ref[pl.ds(pl.multiple_of(b * rb, rb), rb), :]
        return part + m.reshape(rb // SUBLANES, SUBLANES, LANES).sum(axis=0)

    cnt = lax.fori_loop(0, nblk, count_block, jnp.zeros((SUBLANES, LANES), F32)).sum(axis=0, keepdims=True)
    padded = (((cnt.astype(I32) + (MOE_ROW_TILE - 1)) >> shift) << shift)
    start = jnp.zeros((1, LANES), I32)
    tile_expert = jnp.zeros((1, LANES), I32)
    for e in range(n_experts):
        begin = jnp.sum(jnp.where(lane < e, padded, 0), axis=-1, keepdims=True)
        end = jnp.sum(jnp.where(lane <= e, padded, 0), axis=-1, keepdims=True)
        start = start + jnp.where(lane == e, begin, 0)
        tile_expert = tile_expert + jnp.where(lane * MOE_ROW_TILE >= end, 1, 0)
    total = jnp.sum(jnp.where(lane < n_experts, padded, 0), axis=-1, keepdims=True)
    tile_ref[...] = jnp.zeros_like(tile_ref)
    tile_ref[0:1, :] = jnp.minimum(tile_expert, n_experts - 1)
    tile_ref[1:2, :] = jnp.where(lane * MOE_ROW_TILE < total, 1, 0)

    start_f = start.astype(F32)
    strict_lower = (lax.broadcasted_iota(I32, (rb, rb), 0) > lax.broadcasted_iota(I32, (rb, rb), 1)).astype(BF16)
    lane_b = lax.broadcasted_iota(I32, (rb, LANES), 1)
    big = float(2 ** 30)

    def rank_block(b, run):
        r0 = pl.multiple_of(b * rb, rb)
        m = sel_ref[pl.ds(r0, rb), :]
        before = jnp.dot(strict_lower, m.astype(BF16), preferred_element_type=F32) + run
        row = start_f + before
        lo = jnp.min(jnp.where(m > 0.0, row, big), axis=-1, keepdims=True)
        hi = jnp.max(jnp.where(m > 0.0, row, -1.0), axis=-1, keepdims=True)
        comb = comb_ref[pl.ds(r0, rb), :]
        g_lo = jnp.sum(jnp.where((m > 0.0) & (row == lo), comb, 0.0), axis=-1, keepdims=True)
        g_hi = jnp.sum(jnp.where((m > 0.0) & (row == hi), comb, 0.0), axis=-1, keepdims=True)
        gate_ref[pl.ds(r0, rb), :] = jnp.where(lane_b == 0, g_lo, jnp.where(lane_b == 1, g_hi, 0.0))
        rows_t = jnp.where(lane_b == 0, lo, jnp.where(lane_b == 1, hi, 0.0)).T
        dest_ref[:, pl.ds(r0, rb)] = rows_t[0:SUBLANES, :].astype(I32)
        return before[rb - 1:rb, :] + m[rb - 1:rb, :]

    lax.fori_loop(0, nblk, rank_block, jnp.zeros((1, LANES), F32))


def moe_plan(sel, comb, n_experts):
    n_tok = sel.shape[0]
    assert n_tok % PLAN_ROWS == 0
    return pl.pallas_call(
        functools.partial(_moe_plan_kernel, n_experts=n_experts, n_tok=n_tok),
        out_shape=[jax.ShapeDtypeStruct((SUBLANES, n_tok), I32), jax.ShapeDtypeStruct((n_tok, LANES), F32),
                   jax.ShapeDtypeStruct((SUBLANES, LANES), I32)],
        compiler_params=pltpu.CompilerParams(vmem_limit_bytes=VMEM_LIMIT_BYTES),
        name="moe_plan",
    )(sel, comb)


def _row_copy(src_ref, src_row, dst_ref, dst_row, sem):
    return pltpu.make_async_copy(src_ref.at[pl.ds(src_row, 1), :], dst_ref.at[pl.ds(dst_row, 1), :], sem)


def _moe_scatter_kernel(dest_ref, x_ref, init_ref, xs_ref, sem, *, tb):
    del init_ref

    def start(n, carry):
        for k in range(TOP_K):
            _row_copy(x_ref, n, xs_ref, dest_ref[k, n], sem).start()
        return carry

    def wait(n, carry):
        for k in range(TOP_K):
            _row_copy(x_ref, n, xs_ref, dest_ref[k, n], sem).wait()
        return carry

    lax.fori_loop(0, tb, start, 0)
    lax.fori_loop(0, tb, wait, 0)


def moe_scatter(dest, xn, rows):
    n_tok, d = xn.shape
    tb = _pick_tile(n_tok, 256)
    any_spec = pl.BlockSpec(memory_space=pl.ANY)
    return pl.pallas_call(
        functools.partial(_moe_scatter_kernel, tb=tb),
        grid=(n_tok // tb,),
        in_specs=[pl.BlockSpec((SUBLANES, tb), lambda i: (0, i), memory_space=pltpu.SMEM),
                  pl.BlockSpec((tb, d), lambda i: (i, 0)), any_spec],
        out_specs=any_spec,
        out_shape=jax.ShapeDtypeStruct((rows, d), xn.dtype),
        scratch_shapes=[pltpu.SemaphoreType.DMA(())],
        input_output_aliases={2: 0},
        compiler_params=_params("arbitrary"),
        name="moe_scatter",
    )(dest, xn, jnp.zeros((rows, d), xn.dtype))


def _moe_up_kernel(te_ref, tv_ref, a_ref, wg_ref, wu_ref, o_ref, abf_ref):
    i, j = pl.program_id(0), pl.program_id(1)

    @pl.when(j == 0)
    def _():
        abf_ref[...] = a_ref[...].astype(BF16)

    @pl.when(tv_ref[i] == 0)
    def _():
        o_ref[...] = jnp.zeros_like(o_ref)

    @pl.when(tv_ref[i] != 0)
    def _():
        a = abf_ref[...]
        g = jnp.dot(a, wg_ref[...], preferred_element_type=F32)
        u = jnp.dot(a, wu_ref[...], preferred_element_type=F32)
        o_ref[...] = ((g * (1.0 / (1.0 + jnp.exp(-g)))) * u).astype(o_ref.dtype)


def moe_up(tile_expert, tile_valid, xs, wg, wu, tn=512):
    rows, kdim = xs.shape
    f = wg.shape[-1]
    tm, tn = MOE_ROW_TILE, _pick_tile(f, tn)
    w_spec = pl.BlockSpec((None, kdim, tn), lambda i, j, te, tv: (te[i], 0, j))
    return pl.pallas_call(
        _moe_up_kernel,
        grid_spec=pltpu.PrefetchScalarGridSpec(
            num_scalar_prefetch=2,
            grid=(rows // tm, f // tn),
            in_specs=[pl.BlockSpec((tm, kdim), lambda i, j, te, tv: (i, 0)), w_spec, w_spec],
            out_specs=pl.BlockSpec((tm, tn), lambda i, j, te, tv: (i, j)),
            scratch_shapes=[pltpu.VMEM((tm, kdim), BF16)]),
        out_shape=jax.ShapeDtypeStruct((rows, f), BF16),
        compiler_params=_params("arbitrary", "arbitrary"),
        name="moe_up",
    )(tile_expert, tile_valid, xs, wg, wu)


def _moe_down_kernel(te_ref, tv_ref, a_ref, w_ref, o_ref):
    i = pl.program_id(0)

    @pl.when(tv_ref[i] == 0)
    def _():
        o_ref[...] = jnp.zeros_like(o_ref)

    @pl.when(tv_ref[i] != 0)
    def _():
        o_ref[...] = jnp.dot(a_ref[...], w_ref[...], preferred_element_type=F32)


def moe_down(tile_expert, tile_valid, up, wd, tn=512):
    rows, f = up.shape
    d = wd.shape[-1]
    tm, tn = MOE_ROW_TILE, _pick_tile(d, tn)
    return pl.pallas_call(
        _moe_down_kernel,
        grid_spec=pltpu.PrefetchScalarGridSpec(
            num_scalar_prefetch=2,
            grid=(rows // tm, d // tn),
            in_specs=[pl.BlockSpec((tm, f), lambda i, j, te, tv: (i, 0)),
                      pl.BlockSpec((None, f, tn), lambda i, j, te, tv: (te[i], 0, j))],
            out_specs=pl.BlockSpec((tm, tn), lambda i, j, te, tv: (i, j))),
        out_shape=jax.ShapeDtypeStruct((rows, d), F32),
        compiler_params=_params("parallel", "parallel"),
        name="moe_down",
    )(tile_expert, tile_valid, up, wd)


def _moe_combine_kernel(dest_ref, gate_ref, h_ref, y_ref, o_ref, ylo_ref, yhi_ref, sem, *, tb):
    bufs = (ylo_ref, yhi_ref)

    def start(n, carry):
        for k in range(TOP_K):
            _row_copy(y_ref, dest_ref[k, n], bufs[k], n, sem).start()
        return carry

    def wait(n, carry):
        for k in range(TOP_K):
            _row_copy(y_ref, dest_ref[k, n], bufs[k], n, sem).wait()
        return carry

    lax.fori_loop(0, tb, start, 0)
    lax.fori_loop(0, tb, wait, 0)
    g = gate_ref[...]
    o_ref[...] = h_ref[...] + (g[:, 0:1] * ylo_ref[...] + g[:, 1:2] * yhi_ref[...])


def moe_combine(dest, gates, h, y):
    n_tok, d = h.shape
    tb = _pick_tile(n_tok, 256)
    return pl.pallas_call(
        functools.partial(_moe_combine_kernel, tb=tb),
        grid=(n_tok // tb,),
        in_specs=[pl.BlockSpec((SUBLANES, tb), lambda i: (0, i), memory_space=pltpu.SMEM),
                  pl.BlockSpec((tb, LANES), lambda i: (i, 0)),
                  pl.BlockSpec((tb, d), lambda i: (i, 0)),
                  pl.BlockSpec(memory_space=pl.ANY)],
        out_specs=pl.BlockSpec((tb, d), lambda i: (i, 0)),
        out_shape=jax.ShapeDtypeStruct((n_tok, d), F32),
        scratch_shapes=[pltpu.VMEM((tb, d), F32), pltpu.VMEM((tb, d), F32), pltpu.SemaphoreType.DMA(())],
        compiler_params=_params("arbitrary"),
        name="moe_combine",
    )(dest, gates, h, y)


def _dsa_swa_layer(h, hn, w_in, kv_norm, w_uk, w_uv, sinks, w_out, rel_bias, bsz, s_len):
    n_tok, d = h.shape
    a_heads = w_uk.shape[0]
    b_heads = sinks.shape[0]
    qa_w, qb_w = a_heads * HEAD_DIM, b_heads * HEAD_DIM
    kv_w = B_KV_HEADS * HEAD_DIM
    qi_w = IDX_HEADS * IDX_DIM
    widths = (qa_w, A_LATENT, qi_w, IDX_DIM, IDX_HEADS, qb_w, kv_w, kv_w)
    off = np.concatenate([[0], np.cumsum(widths)])
    seg = lambda s: w_in[:, int(off[s]):int(off[s + 1])]
    tail_pad = LANES - IDX_DIM - IDX_HEADS
    w_perm = jnp.concatenate([seg(0), seg(5), seg(6), seg(7), seg(1), seg(2), seg(3), seg(4),
                              jnp.zeros((d, tail_pad), w_in.dtype)], axis=1).astype(BF16)
    assert qa_w == qb_w and (qa_w + qb_w) % (2 * kv_w) == 0
    assert (qa_w + qb_w + 2 * kv_w) % A_LATENT == 0 and (qa_w + qb_w + 2 * kv_w + A_LATENT) % qi_w == 0
    kvb_block = (qa_w + qb_w) // (2 * kv_w)
    ckv_block = (qa_w + qb_w + 2 * kv_w) // A_LATENT
    qi_block = (qa_w + qb_w + 2 * kv_w + A_LATENT) // qi_w
    kw_block = (qa_w + qb_w + 2 * kv_w + A_LATENT + qi_w) // LANES

    proj = matmul(hn, w_perm, out_dtype=F32, tn=896)
    proj3 = proj.reshape(bsz, s_len, -1)
    topk = min(TOPK_MAX, s_len // 4)

    packed = latent_pack(proj, ckv_block, kv_norm, RMS_EPS)
    planes = packed.shape[0]
    ckv = jnp.transpose(packed, (1, 0, 2)).reshape(bsz, planes * s_len, LANES)
    idx = dsa_index(proj3, qi_block, kw_block, topk, row_scale=planes)
    q_lat = head_matmul(proj, w_uk.astype(BF16), out_dtype=BF16)
    bias_t = jnp.pad(rel_bias[:, :a_heads].T.astype(F32), ((0, 0), (0, LANES - NUM_BUCKETS)))
    o_lat = dsa_attention(idx, q_lat.reshape(bsz, s_len, a_heads, A_LATENT), ckv, bias_t)
    o_a = head_matmul(o_lat.reshape(n_tok, a_heads * A_LATENT), w_uv.astype(BF16), out_dtype=BF16)

    o_b = swa_attention(proj3, 1, kvb_block, rel_bias.astype(F32), sinks.astype(F32), a_heads)
    o = jnp.concatenate([o_a, o_b.reshape(n_tok, qb_w)], axis=1)
    return matmul(o, w_out.astype(BF16), out_dtype=F32, res=h)


def _diff_layer(h, hn, w_in, lam_vecs, subln, w_out, rel_bias, lambda_init, bsz, s_len):
    n_tok = h.shape[0]
    nheads = w_out.shape[0] // subln.shape[0]
    proj = matmul(hn, w_in.astype(BF16), out_dtype=BF16)
    o = diff_attention(proj.reshape(bsz, s_len, -1), rel_bias.astype(F32), lam_vecs.astype(F32),
                       subln.astype(F32), nheads, lambda_init)
    return matmul(o.reshape(n_tok, -1), w_out.astype(BF16), out_dtype=F32, res=h)


def _moe_layer(h, g, w_router, w_gate, w_up, w_down):
    n_tok, n_experts = h.shape[0], w_router.shape[1]
    xn, comb, sel = norm_router(h, g, w_router, RMS_EPS)
    dest, gates, tiles = moe_plan(sel, comb, n_experts)
    assert (TOP_K * n_tok) % MOE_ROW_TILE == 0
    rows = TOP_K * n_tok + n_experts * MOE_ROW_TILE
    assert rows // MOE_ROW_TILE <= LANES
    xs = moe_scatter(dest, xn, rows)
    up = moe_up(tiles[0], tiles[1], xs, w_gate.astype(BF16), w_up.astype(BF16))
    y = moe_down(tiles[0], tiles[1], up, w_down.astype(BF16))
    return moe_combine(dest, gates, h, y)


def kernel(x, ln_mix, ln_ffn, rel_bias, w_in_ab, kv_norm, w_uk, w_uv, sinks, w_out_ab, w_gate_d, w_up_d,
           w_down_d, w_in_c, lambda_q1, lambda_k1, lambda_q2, lambda_k2, subln, w_out_c, w_router,
           w_gate_e, w_up_e, w_down_e, ln_final):
    bsz, s_len, d = x.shape
    h = x.reshape(bsz * s_len, d)
    for i in range(ln_mix.shape[0]):
        j = i // 2
        hn = rmsnorm(h, ln_mix[i], RMS_EPS, BF16)
        if i % 2 == 0:
            h = _dsa_swa_layer(h, hn, w_in_ab[j], kv_norm[j], w_uk[j], w_uv[j], sinks[j], w_out_ab[j],
                               rel_bias, bsz, s_len)
            hf = rmsnorm(h, ln_ffn[i], RMS_EPS, BF16)
            up = swiglu_up(hf, w_gate_d[j].astype(BF16), w_up_d[j].astype(BF16))
            h = matmul(up, w_down_d[j].astype(BF16), out_dtype=F32, res=h)
        else:
            lambda_init = 0.8 - 0.6 * math.exp(-0.3 * i)
            lam_vecs = jnp.stack([lambda_q1[j], lambda_k1[j], lambda_q2[j], lambda_k2[j]])
            h = _diff_layer(h, hn, w_in_c[j], lam_vecs, subln[j], w_out_c[j], rel_bias, lambda_init,
                            bsz, s_len)
            h = _moe_layer(h, ln_ffn[i], w_router[j], w_gate_e[j], w_up_e[j], w_down_e[j])
    return rmsnorm(h, ln_final, RMS_EPS, x.dtype).reshape(bsz, s_len, d)
```

```python
import functools
import math

import numpy as np
import jax
import jax.numpy as jnp
from jax import lax
from jax.experimental import pallas as pl
from jax.experimental.pallas import tpu as pltpu

F32 = jnp.float32
BF16 = jnp.bfloat16
I32 = jnp.int32

HEAD_DIM = 128
A_LATENT = 512
IDX_HEADS = 16
IDX_DIM = 64
TOPK_MAX = 256
B_KV_HEADS = 2
WINDOW = 128
NUM_BUCKETS = 32
MAX_DISTANCE = 128
TOP_K = 2
RMS_EPS = 1e-6
SUBLN_EPS = 1e-5

LANES = 128
SUBLANES = 8
VMEM_LIMIT_BYTES = 56 * 1024 * 1024
INT_MIN = -(2 ** 31)
NEG_INF = float("-inf")

GATHER_STRIDE = TOPK_MAX + SUBLANES
LATENT_PLANES = A_LATENT // (2 * LANES)
RANKS_PER_STEP = 4
CHUNK_GROUP = 4


def _bucket_thresholds():
    n = np.arange(0, 4 * MAX_DISTANCE, dtype=np.int64)
    max_exact = NUM_BUCKETS // 2
    nf = np.maximum(n, 1).astype(np.float32)
    large = max_exact + (
        np.log(nf / np.float32(max_exact)) / np.float32(math.log(MAX_DISTANCE / max_exact))
        * np.float32(NUM_BUCKETS - max_exact)
    ).astype(np.int32)
    large = np.minimum(large, NUM_BUCKETS - 1)
    bucket = np.where(n < max_exact, n, large)
    assert np.all(np.diff(bucket) >= 0) and bucket[-1] == NUM_BUCKETS - 1
    return tuple(int(np.argmax(bucket >= k)) for k in range(1, NUM_BUCKETS))


BUCKET_THRESHOLDS = _bucket_thresholds()
FAR_DISTANCE = BUCKET_THRESHOLDS[-1]


def _bias_from_distance(dist, table_entry):
    val = jnp.where(dist >= BUCKET_THRESHOLDS[0], table_entry(1), table_entry(0))
    for k in range(2, NUM_BUCKETS):
        val = jnp.where(dist >= BUCKET_THRESHOLDS[k - 1], table_entry(k), val)
    return val


def _pick_tile(dim, target, align=LANES):
    if dim <= target:
        return dim
    t = (target // align) * align
    while t >= align:
        if dim % t == 0:
            return t
        t -= align
    return dim


def _params(*semantics):
    return pltpu.CompilerParams(dimension_semantics=semantics, vmem_limit_bytes=VMEM_LIMIT_BYTES)


def _rmsnorm_kernel(x_ref, g_ref, o_ref, *, eps):
    x = x_ref[...].astype(F32)
    y = x * lax.rsqrt(jnp.mean(x * x, axis=-1, keepdims=True) + eps)
    o_ref[...] = (y * g_ref[...].astype(F32)).astype(o_ref.dtype)


def rmsnorm(x2d, g, eps, out_dtype):
    m, d = x2d.shape
    tm = _pick_tile(m, 256, SUBLANES)
    return pl.pallas_call(
        functools.partial(_rmsnorm_kernel, eps=eps),
        grid=(m // tm,),
        in_specs=[pl.BlockSpec((tm, d), lambda i: (i, 0)), pl.BlockSpec((1, d), lambda i: (0, 0))],
        out_specs=pl.BlockSpec((tm, d), lambda i: (i, 0)),
        out_shape=jax.ShapeDtypeStruct((m, d), out_dtype),
        compiler_params=_params("parallel"),
        name="rmsnorm",
    )(x2d, g.reshape(1, d))


def _matmul_kernel(*refs, nk, has_res, scale_col):
    a_ref, w_ref = refs[0], refs[1]
    pos = 2
    res_ref = scale_ref = None
    if has_res:
        res_ref = refs[pos]
        pos += 1
    if scale_col is not None:
        scale_ref = refs[pos]
        pos += 1
    o_ref = refs[pos]
    k = pl.program_id(2)

    def finish(y):
        if scale_ref is not None:
            y = y * scale_ref[:, scale_col:scale_col + 1]
        if res_ref is not None:
            y = res_ref[...].astype(F32) + y
        o_ref[...] = y.astype(o_ref.dtype)

    part = jnp.dot(a_ref[...], w_ref[...], preferred_element_type=F32)
    if nk == 1:
        finish(part)
        return
    acc_ref = refs[pos + 1]

    @pl.when(k == 0)
    def _():
        acc_ref[...] = part

    @pl.when(k > 0)
    def _():
        acc_ref[...] += part

    @pl.when(k == nk - 1)
    def _():
        finish(acc_ref[...])


def matmul(a, w, *, out_dtype, res=None, row_scale=None, scale_col=None, expert=None,
           tm=1024, tn=512, tk=4096):
    m, kdim = a.shape
    n = w.shape[-1]
    tm, tn, tk = _pick_tile(m, tm, SUBLANES), _pick_tile(n, tn), _pick_tile(kdim, tk)
    nk = kdim // tk
    if expert is None:
        w_spec = pl.BlockSpec((tk, tn), lambda i, j, k: (k, j))
    else:
        w_spec = pl.BlockSpec((None, tk, tn), lambda i, j, k: (expert, k, j))
    in_specs = [pl.BlockSpec((tm, tk), lambda i, j, k: (i, k)), w_spec]
    args = [a, w]
    if res is not None:
        in_specs.append(pl.BlockSpec((tm, tn), lambda i, j, k: (i, j)))
        args.append(res)
    if row_scale is not None:
        in_specs.append(pl.BlockSpec((tm, row_scale.shape[1]), lambda i, j, k: (i, 0)))
        args.append(row_scale)
    return pl.pallas_call(
        functools.partial(_matmul_kernel, nk=nk, has_res=res is not None,
                          scale_col=scale_col if row_scale is not None else None),
        grid=(m // tm, n // tn, nk),
        in_specs=in_specs,
        out_specs=pl.BlockSpec((tm, tn), lambda i, j, k: (i, j)),
        out_shape=jax.ShapeDtypeStruct((m, n), out_dtype),
        scratch_shapes=[pltpu.VMEM((tm, tn), F32)] if nk > 1 else [],
        compiler_params=_params("parallel", "parallel", "arbitrary"),
        name="matmul",
    )(*args)


def _swiglu_up_kernel(a_ref, wg_ref, wu_ref, o_ref, *acc_refs, nk):
    k = pl.program_id(2)

    def finish(g, u):
        silu = g * (1.0 / (1.0 + jnp.exp(-g)))
        o_ref[...] = (silu * u).astype(o_ref.dtype)

    a = a_ref[...]
    gate = jnp.dot(a, wg_ref[...], preferred_element_type=F32)
    up = jnp.dot(a, wu_ref[...], preferred_element_type=F32)
    if nk == 1:
        finish(gate, up)
        return
    accg_ref, accu_ref = acc_refs

    @pl.when(k == 0)
    def _():
        accg_ref[...] = gate
        accu_ref[...] = up

    @pl.when(k > 0)
    def _():
        accg_ref[...] += gate
        accu_ref[...] += up

    @pl.when(k == nk - 1)
    def _():
        finish(accg_ref[...], accu_ref[...])


def swiglu_up(a, wg, wu, *, expert=None, tm=1024, tn=512, tk=4096):
    m, kdim = a.shape
    f = wg.shape[-1]
    tm, tn, tk = _pick_tile(m, tm, SUBLANES), _pick_tile(f, tn), _pick_tile(kdim, tk)
    nk = kdim // tk
    if expert is None:
        w_spec = pl.BlockSpec((tk, tn), lambda i, j, k: (k, j))
    else:
        w_spec = pl.BlockSpec((None, tk, tn), lambda i, j, k: (expert, k, j))
    return pl.pallas_call(
        functools.partial(_swiglu_up_kernel, nk=nk),
        grid=(m // tm, f // tn, nk),
        in_specs=[pl.BlockSpec((tm, tk), lambda i, j, k: (i, k)), w_spec, w_spec],
        out_specs=pl.BlockSpec((tm, tn), lambda i, j, k: (i, j)),
        out_shape=jax.ShapeDtypeStruct((m, f), BF16),
        scratch_shapes=[pltpu.VMEM((tm, tn), F32), pltpu.VMEM((tm, tn), F32)] if nk > 1 else [],
        compiler_params=_params("parallel", "parallel", "arbitrary"),
        name="swiglu_up",
    )(a, wg, wu)


def _head_matmul_kernel(a_ref, w_ref, o_ref):
    o_ref[...] = jnp.dot(a_ref[...].astype(BF16), w_ref[...], preferred_element_type=F32).astype(o_ref.dtype)


def head_matmul(a, w3, *, out_dtype, tm=1024):
    m = a.shape[0]
    nheads, ka, nb = w3.shape
    tm = _pick_tile(m, tm, SUBLANES)
    return pl.pallas_call(
        _head_matmul_kernel,
        grid=(m // tm, nheads),
        in_specs=[pl.BlockSpec((tm, ka), lambda i, h: (i, h)),
                  pl.BlockSpec((None, ka, nb), lambda i, h: (h, 0, 0))],
        out_specs=pl.BlockSpec((tm, nb), lambda i, h: (i, h)),
        out_shape=jax.ShapeDtypeStruct((m, nheads * nb), out_dtype),
        compiler_params=_params("parallel", "parallel"),
        name="head_matmul",
    )(a, w3)


def _bits(x):
    return lax.bitcast_convert_type(x, I32)


def _latent_pack_kernel(c_ref, g_ref, o_ref, *, eps):
    x = c_ref[...]
    y = x * lax.rsqrt(jnp.mean(x * x, axis=-1, keepdims=True) + eps) * g_ref[...]
    yb = _bits(y.astype(BF16).astype(F32))
    for r in range(A_LATENT // (2 * LANES)):
        lo = lax.shift_right_logical(yb[:, (2 * r) * LANES:(2 * r + 1) * LANES], 16)
        hi = yb[:, (2 * r + 1) * LANES:(2 * r + 2) * LANES] & jnp.int32(-65536)
        o_ref[r] = lo | hi


def latent_pack(proj, col_block, g, eps):
    m = proj.shape[0]
    tm = _pick_tile(m, 512, SUBLANES)
    planes = A_LATENT // (2 * LANES)
    return pl.pallas_call(
        functools.partial(_latent_pack_kernel, eps=eps),
        grid=(m // tm,),
        in_specs=[pl.BlockSpec((tm, A_LATENT), lambda i: (i, col_block)),
                  pl.BlockSpec((1, A_LATENT), lambda i: (0, 0))],
        out_specs=pl.BlockSpec((planes, tm, LANES), lambda i: (0, i, 0)),
        out_shape=jax.ShapeDtypeStruct((planes, m, LANES), I32),
        compiler_params=_params("parallel"),
        name="latent_pack",
    )(proj, g.reshape(1, A_LATENT))


def _dsa_index_kernel(qi_ref, kw_ref, kwq_ref, idx_ref, qt_ref, key_ref, rank_ref, acc_ref, run_ref, bnd_ref,
                      *, tq, topk, row_scale):
    ch = tq
    i = pl.program_id(1)
    nchunks = i + 1

    q_t = qi_ref[0].T
    for h in range(IDX_HEADS):
        qt_ref[:, h * tq:(h + 1) * tq] = q_t[h * IDX_DIM:(h + 1) * IDX_DIM, :].astype(BF16)
    w_t = kwq_ref[0].T[IDX_DIM:IDX_DIM + IDX_HEADS, :] * (IDX_DIM ** -0.5 * IDX_HEADS ** -0.5)
    t_row = i * tq + lax.broadcasted_iota(I32, (1, tq), 1)
    s_iota = lax.broadcasted_iota(I32, (ch, tq), 0)

    def chunk_start(c):
        return pl.multiple_of(c * ch, ch)

    group = CHUNK_GROUP
    ngroups = (nchunks + (group - 1)) >> (group.bit_length() - 1)

    def score_group(g, carry):
        rows = group * ch
        g0 = pl.multiple_of(g * rows, rows)
        kc = kw_ref[0, pl.ds(g0, rows), :][:, :IDX_DIM].astype(BF16)
        dots = jnp.dot(kc, qt_ref[...], preferred_element_type=F32)
        for u in range(group):
            sc = jnp.zeros((ch, tq), F32)
            for h in range(IDX_HEADS):
                sc = sc + jnp.maximum(dots[u * ch:(u + 1) * ch, h * tq:(h + 1) * tq], 0.0) * w_t[h:h + 1, :]
            bits = _bits(sc)
            key = bits ^ ((bits >> 31) & jnp.int32(0x7FFFFFFF))
            r0 = g0 + u * ch
            key_ref[pl.ds(r0, ch), :] = jnp.where(r0 + s_iota <= t_row, key, jnp.int32(INT_MIN))
        return carry

    lax.fori_loop(0, ngroups, score_group, 0)

    def column_sum(m):
        return m.reshape(m.shape[0] // SUBLANES, SUBLANES, tq).sum(axis=0)

    def count(preds):
        rows = group * ch

        def body(g, parts):
            k = key_ref[pl.ds(pl.multiple_of(g * rows, rows), rows), :]
            return tuple(part + column_sum(jnp.where(pred(k), 1, 0).astype(I32)) for part, pred in zip(parts, preds))
        parts = lax.fori_loop(0, ngroups, body, tuple(jnp.zeros((SUBLANES, tq), I32) for _ in preds))
        return tuple(part.sum(axis=0, keepdims=True) for part in parts)

    def at_least(cand):
        return lambda k: k >= cand

    thr = jnp.where(count((at_least(0),))[0] >= topk, 0, INT_MIN).astype(I32)

    def next_bit(b, thr):
        cand = thr | lax.shift_left(jnp.int32(1), 30 - b)
        return jnp.where(count((at_least(cand),))[0] >= topk, cand, thr)

    thr = lax.fori_loop(0, 31, next_bit, thr)
    n_gt, n_ge = count((lambda k: k > thr, lambda k: (k >= thr) & (k > INT_MIN)))
    need = (topk - n_gt).astype(F32)
    has_ties = jnp.max(n_ge) > topk

    tri = (lax.broadcasted_iota(I32, (ch, ch), 0) >= lax.broadcasted_iota(I32, (ch, ch), 1)).astype(BF16)

    def select_chunk(with_ties, c, carry):
        run, eq_run = carry
        r0 = chunk_start(c)
        k = key_ref[pl.ds(r0, ch), :]
        if with_ties:
            eq = jnp.where(k == thr, jnp.where(k > INT_MIN, 1.0, 0.0), 0.0)
            eq_cum = jnp.dot(tri, eq.astype(BF16), preferred_element_type=F32) + eq_run
            sel = jnp.where(k > thr, 1.0, jnp.where(eq_cum <= need, eq, 0.0))
            eq_run = eq_cum[ch - 1:ch, :]
        else:
            sel = jnp.where(k >= thr, jnp.where(k > INT_MIN, 1.0, 0.0), 0.0)
        rank_ref[pl.ds(r0, ch), :] = sel * jnp.dot(tri, sel.astype(BF16), preferred_element_type=F32)
        n_sel = column_sum(sel).sum(axis=0, keepdims=True)
        run_ref[c] = run
        bnd_ref[0, c] = jnp.min(run).astype(I32)
        bnd_ref[1, c] = jnp.max(run).astype(I32)
        bnd_ref[2, c] = jnp.max(n_sel).astype(I32)
        return run + n_sel, eq_run

    zero_row = jnp.zeros((1, tq), F32)

    def select_group(with_ties, g, carry):
        for u in range(group):
            carry = select_chunk(with_ties, g * group + u, carry)
        return carry

    @pl.when(has_ties)
    def _():
        lax.fori_loop(0, ngroups, functools.partial(select_group, True), (zero_row, zero_row))

    @pl.when(jnp.logical_not(has_ties))
    def _():
        lax.fori_loop(0, ngroups, functools.partial(select_group, False), (zero_row, zero_row))

    acc_ref[...] = jnp.zeros_like(acc_ref)
    sub_iota = lax.broadcasted_iota(I32, (SUBLANES, tq), 0)
    last_tile = topk // SUBLANES - 1

    def compact_chunk(c, carry):
        r0 = chunk_start(c)
        run_lo, run_hi = bnd_ref[0, c], bnd_ref[1, c]
        run = run_ref[c]
        key_index_1 = (r0 + 1 + s_iota).astype(F32)

        def place_ranks(g, carry):
            rank = rank_ref[pl.ds(r0, ch), :]
            first = g * RANKS_PER_STEP
            slots, keys = [], []
            for u in range(RANKS_PER_STEP):
                r = (first + u).astype(F32)
                hit = jnp.where(rank == r + 1.0, key_index_1, 0.0)
                found = column_sum(hit).sum(axis=0, keepdims=True)
                slots.append(jnp.where(found > 0.0, run + r, -1.0))
                keys.append(found - 1.0)

            def place_tile(t8, carry):
                j0 = pl.multiple_of(t8 * SUBLANES, SUBLANES)
                rows = (j0 + sub_iota).astype(F32)
                cur = acc_ref[pl.ds(j0, SUBLANES), :]
                for slot, key in zip(slots, keys):
                    cur = jnp.where(rows == slot, key, cur)
                acc_ref[pl.ds(j0, SUBLANES), :] = cur
                return carry

            hi_tile = jnp.minimum((run_hi + first + (RANKS_PER_STEP - 1)) >> 3, last_tile)
            lax.fori_loop((run_lo + first) >> 3, hi_tile + 1, place_tile, 0)
            return carry

        steps = (bnd_ref[2, c] + (RANKS_PER_STEP - 1)) >> (RANKS_PER_STEP.bit_length() - 1)
        lax.fori_loop(0, steps, place_ranks, 0)
        return carry

    lax.fori_loop(0, nchunks, compact_chunk, 0)
    idx_ref[0] = (acc_ref[...] * float(row_scale)).T.astype(I32)


def dsa_index(proj3, qi_block, kw_block, topk, row_scale=1, tq=128):
    bsz, s_len, _ = proj3.shape
    tq = min(tq, s_len)
    nq = s_len // tq
    assert nq % CHUNK_GROUP == 0
    return pl.pallas_call(
        functools.partial(_dsa_index_kernel, tq=tq, topk=topk, row_scale=row_scale),
        grid=(bsz, nq),
        in_specs=[pl.BlockSpec((1, tq, IDX_HEADS * IDX_DIM), lambda b, i: (b, i, qi_block)),
                  pl.BlockSpec((1, s_len, LANES), lambda b, i: (b, 0, kw_block)),
                  pl.BlockSpec((1, tq, LANES), lambda b, i: (b, i, kw_block))],
        out_specs=pl.BlockSpec((1, tq, topk), lambda b, i: (b, i, 0)),
        out_shape=jax.ShapeDtypeStruct((bsz, s_len, topk), I32),
        scratch_shapes=[pltpu.VMEM((IDX_DIM, IDX_HEADS * tq), BF16),
                        pltpu.VMEM((s_len, tq), I32),
                        pltpu.VMEM((s_len, tq), F32),
                        pltpu.VMEM((topk, tq), F32),
                        pltpu.VMEM((nq, 1, tq), F32),
                        pltpu.SMEM((3, nq), I32)],
        compiler_params=_params("parallel", "arbitrary"),
        name="dsa_index",
    )(proj3, proj3, proj3)


def _dsa_attn_kernel(idx_s_ref, idx_v_ref, ql_ref, ckv_ref, tab_ref, o_ref, *tile_refs, tq, topk, nheads):
    i = pl.program_id(1)
    planes = LATENT_PLANES
    plane_shift = planes.bit_length() - 1
    assert planes == 1 << plane_shift
    lane_iota = lax.broadcasted_iota(I32, (nheads, topk), 1)
    scale = HEAD_DIM ** -0.5

    def gather(q, tile_ref, lo, hi):
        for mi in range(lo, hi):
            row = pl.multiple_of(idx_s_ref[0, q, mi], planes)
            tile_ref[pl.ds(mi, planes, stride=GATHER_STRIDE), :] = ckv_ref[0, pl.ds(row, planes), :]

    def attend(queries, tiles):
        nq = len(queries)
        xs = [[] for _ in queries]
        for r in range(planes):
            for n in range(nq):
                w = tiles[n][r * GATHER_STRIDE:r * GATHER_STRIDE + topk, :]
                xs[n].append(lax.bitcast_convert_type(lax.shift_left(w, 16), F32).astype(BF16))
                xs[n].append(lax.bitcast_convert_type(w & jnp.int32(-65536), F32).astype(BF16))
            yield
        logits = []
        for n, q in enumerate(queries):
            ql = ql_ref[0, q]
            acc = jnp.zeros((nheads, topk), F32)
            for j, x in enumerate(xs[n]):
                acc = acc + lax.dot_general(ql[:, j * LANES:(j + 1) * LANES], x, (((1,), (1,)), ((), ())),
                                            preferred_element_type=F32)
            logits.append(acc)
        yield
        bias = []
        for q in queries:
            dist = jnp.broadcast_to(i * tq + q - (idx_v_ref[0, q] >> plane_shift), (nheads, topk))
            bias.append(_bias_from_distance(dist, lambda k: tab_ref[:, k:k + 1]))
        yield
        es, inv = [], []
        for n, q in enumerate(queries):
            z = jnp.where(lane_iota < jnp.minimum(i * tq + q + 1, topk), logits[n] * scale + bias[n], NEG_INF)
            e = jnp.exp(z - jnp.max(z, axis=-1, keepdims=True))
            inv.append(1.0 / jnp.sum(e, axis=-1, keepdims=True))
            es.append(e.astype(BF16))
        yield
        for j in range(2 * planes):
            for n, q in enumerate(queries):
                o = jnp.dot(es[n], xs[n][j], preferred_element_type=F32) * inv[n]
                o_ref[0, q, :, j * LANES:(j + 1) * LANES] = o.astype(o_ref.dtype)
            if j % 2 == 1:
                yield

    n_stages = 2 * planes + 3
    rows_per_stage = topk // n_stages

    def attend_while_gathering(queries, tiles, next_queries, next_tiles):
        stages = attend(queries, tiles)
        for s in range(n_stages):
            next(stages)
            hi = topk if s == n_stages - 1 else (s + 1) * rows_per_stage
            for q, tile_ref in zip(next_queries, next_tiles):
                gather(q, tile_ref, s * rows_per_stage, hi)

    half = len(tile_refs) // 2
    first, second = tile_refs[:half], tile_refs[half:]
    for n in range(half):
        gather(n, first[n], 0, topk)

    def query_group(qg, carry):
        q0 = 2 * half * qg
        qa = [q0 + n for n in range(half)]
        qb = [q0 + half + n for n in range(half)]
        qc = [jnp.minimum(q0 + 2 * half + n, tq - 1) for n in range(half)]
        attend_while_gathering(qa, first, qb, second)
        attend_while_gathering(qb, second, qc, first)
        return carry

    lax.fori_loop(0, tq // (2 * half), query_group, 0)


def dsa_attention(idx, q_lat, ckv_packed, bias_t, tq=64):
    bsz, s_len, topk = idx.shape
    nheads = q_lat.shape[2]
    tq = min(tq, s_len)
    planes = LATENT_PLANES
    return pl.pallas_call(
        functools.partial(_dsa_attn_kernel, tq=tq, topk=topk, nheads=nheads),
        grid=(bsz, s_len // tq),
        in_specs=[pl.BlockSpec((1, tq, topk), lambda b, i: (b, i, 0), memory_space=pltpu.SMEM),
                  pl.BlockSpec((1, tq, 1, topk), lambda b, i: (b, i, 0, 0)),
                  pl.BlockSpec((1, tq, nheads, A_LATENT), lambda b, i: (b, i, 0, 0)),
                  pl.BlockSpec((1, planes * s_len, LANES), lambda b, i: (b, 0, 0)),
                  pl.BlockSpec((nheads, LANES), lambda b, i: (0, 0))],
        out_specs=pl.BlockSpec((1, tq, nheads, A_LATENT), lambda b, i: (b, i, 0, 0)),
        out_shape=jax.ShapeDtypeStruct((bsz, s_len, nheads, A_LATENT), BF16),
        scratch_shapes=[pltpu.VMEM((planes * GATHER_STRIDE, LANES), I32)] * 8,
        compiler_params=_params("parallel", "arbitrary"),
        name="dsa_attention",
    )(idx, idx.reshape(bsz, s_len, 1, topk), q_lat, ckv_packed, bias_t)


def _swa_kernel(tab_ref, sink_ref, q_ref, kvp_ref, kvc_ref, o_ref, band_ref, *, nheads, slot0):
    w = WINDOW
    group = nheads // B_KV_HEADS
    b, n = pl.program_id(0), pl.program_id(1)
    rel = (lax.broadcasted_iota(I32, (w, 2 * w), 0) + w) - lax.broadcasted_iota(I32, (w, 2 * w), 1)

    @pl.when((b == 0) & (n == 0))
    def _():
        for hb in range(nheads):
            band_ref[hb] = _bias_from_distance(rel, lambda k: tab_ref[k, slot0 + hb])

    s_global = n * w - w + lax.broadcasted_iota(I32, (w, 2 * w), 1)
    mask = (rel >= 0) & (rel < w) & (s_global >= 0)
    scale = HEAD_DIM ** -0.5
    kv_width = B_KV_HEADS * HEAD_DIM
    for kh in range(B_KV_HEADS):
        ks = slice(kh * HEAD_DIM, (kh + 1) * HEAD_DIM)
        vs = slice(kv_width + kh * HEAD_DIM, kv_width + (kh + 1) * HEAD_DIM)
        kk = jnp.concatenate([kvp_ref[0][:, ks], kvc_ref[0][:, ks]], axis=0).astype(BF16)
        vv = jnp.concatenate([kvp_ref[0][:, vs], kvc_ref[0][:, vs]], axis=0).astype(BF16)
        for g in range(group):
            hb = kh * group + g
            cols = slice(hb * HEAD_DIM, (hb + 1) * HEAD_DIM)
            q = q_ref[0][:, cols].astype(BF16)
            z = lax.dot_general(q, kk, (((1,), (1,)), ((), ())), preferred_element_type=F32) * scale + band_ref[hb]
            z = jnp.where(mask, z, NEG_INF)
            sink = sink_ref[hb]
            m = jnp.maximum(jnp.max(z, axis=-1, keepdims=True), sink)
            e = jnp.exp(z - m)
            den = jnp.sum(e, axis=-1, keepdims=True) + jnp.exp(sink - m)
            o_ref[0, :, cols] = jnp.dot((e / den).astype(BF16), vv, preferred_element_type=F32).astype(o_ref.dtype)


def swa_attention(proj3, q_block, kv_block, rel_bias, sinks, slot0):
    bsz, s_len, _ = proj3.shape
    nheads = sinks.shape[0]
    qw = nheads * HEAD_DIM
    kvw = 2 * B_KV_HEADS * HEAD_DIM
    w = WINDOW
    smem = pl.BlockSpec(memory_space=pltpu.SMEM)
    return pl.pallas_call(
        functools.partial(_swa_kernel, nheads=nheads, slot0=slot0),
        grid=(bsz, s_len // w),
        in_specs=[smem, smem,
                  pl.BlockSpec((1, w, qw), lambda b, n: (b, n, q_block)),
                  pl.BlockSpec((1, w, kvw), lambda b, n: (b, jnp.maximum(n - 1, 0), kv_block)),
                  pl.BlockSpec((1, w, kvw), lambda b, n: (b, n, kv_block))],
        out_specs=pl.BlockSpec((1, w, qw), lambda b, n: (b, n, 0)),
        out_shape=jax.ShapeDtypeStruct((bsz, s_len, qw), BF16),
        scratch_shapes=[pltpu.VMEM((nheads, w, 2 * w), F32)],
        compiler_params=_params("arbitrary", "arbitrary"),
        name="swa_attention",
    )(rel_bias, sinks, proj3, proj3, proj3)


def _diff_attn_kernel(tab_ref, lam_ref, subln_ref, q1_ref, q2_ref, k1_ref, k2_ref, v_ref, o_ref,
                      band_ref, m_ref, l_ref, acc_ref, *, tq, nheads, lambda_init):
    h, qi = pl.program_id(1), pl.program_id(2)
    ii = lax.broadcasted_iota(I32, (tq, tq), 0)
    jj = lax.broadcasted_iota(I32, (tq, tq), 1)
    log2e = 1.4426950408889634
    qk_scale = HEAD_DIM ** -0.5 * log2e
    last = NUM_BUCKETS - 1
    ntile = tq // LANES
    vd = acc_ref.shape[-1]

    @pl.when(qi == 0)
    def _():
        ti = lax.broadcasted_iota(I32, (LANES, LANES), 0)
        tj = lax.broadcasted_iota(I32, (LANES, LANES), 1)
        band_ref[...] = jnp.zeros_like(band_ref)
        for m in range(2):
            slot = m * nheads + h
            far = tab_ref[last, slot]
            for which, offset in ((0, 0), (1, tq)):
                for r in range(ntile):
                    for c in range(ntile):
                        base = (r - c) * LANES + offset
                        if base + (LANES - 1) < 0 or base - (LANES - 1) >= FAR_DISTANCE:
                            continue
                        tile = _bias_from_distance(ti - tj + base, lambda k: tab_ref[k, slot])
                        band_ref[m, which, r * LANES:(r + 1) * LANES, c * LANES:(c + 1) * LANES] = (tile - far) * log2e

    m_ref[...] = jnp.full_like(m_ref, NEG_INF)
    l_ref[...] = jnp.zeros_like(l_ref)
    acc_ref[...] = jnp.zeros_like(acc_ref)
    qs = (q1_ref[0], q2_ref[0])
    k_refs = (k1_ref, k2_ref)

    def block(kj, band_idx, causal):
        r0 = pl.multiple_of(kj * tq, tq)
        v = v_ref[0, pl.ds(r0, tq), :]
        scores = [lax.dot_general(qs[m], k_refs[m][0, pl.ds(r0, tq), :], (((1,), (1,)), ((), ())),
                                  preferred_element_type=F32) for m in range(2)]
        alphas, pvs = [], []
        for m in range(2):
            z = scores[m] * qk_scale
            if band_idx is not None:
                z = z + band_ref[m, band_idx]
            if causal:
                z = jnp.where(jj <= ii, z, NEG_INF)
            zt = [z[:, j * LANES:(j + 1) * LANES] for j in range(ntile)]
            m_old = m_ref[m]
            m_new = jnp.maximum(m_old, jnp.max(functools.reduce(jnp.maximum, zt), axis=-1, keepdims=True))
            alpha = jnp.exp2(m_old - m_new)
            es = [jnp.exp2(t - m_new) for t in zt]
            l_ref[m] = alpha * l_ref[m] + functools.reduce(jnp.add, es)
            m_ref[m] = m_new
            alphas.append(alpha)
            pvs.append(jnp.dot(jnp.concatenate([e.astype(BF16) for e in es], axis=1), v,
                               preferred_element_type=F32))
        for m in range(2):
            for c in range(vd // LANES):
                cols = slice(c * LANES, (c + 1) * LANES)
                acc_ref[m, :, cols] = alphas[m] * acc_ref[m, :, cols] + pvs[m][:, cols]

    def far_block(kj, carry):
        block(kj, None, causal=False)
        return carry

    lax.fori_loop(0, jnp.maximum(qi - 1, 0), far_block, 0)

    @pl.when(qi >= 1)
    def _():
        block(qi - 1, 1, causal=False)

    block(qi, 0, causal=True)

    lam = (jnp.exp(jnp.sum(lam_ref[0:1, :] * lam_ref[1:2, :], axis=-1, keepdims=True))
           - jnp.exp(jnp.sum(lam_ref[2:3, :] * lam_ref[3:4, :], axis=-1, keepdims=True)) + lambda_init)
    outs = [acc_ref[m] / jnp.sum(l_ref[m], axis=-1, keepdims=True) for m in range(2)]
    o = outs[0] - lam * outs[1]
    y = o * lax.rsqrt(jnp.mean(o * o, axis=-1, keepdims=True) + SUBLN_EPS)
    o_ref[0] = ((y * subln_ref[...]) * (1.0 - lambda_init)).astype(o_ref.dtype)


def diff_attention(proj3, rel_bias, lam_vecs, subln, nheads, lambda_init, tq=1024):
    bsz, s_len, _ = proj3.shape
    tq = min(tq, s_len)
    assert tq > FAR_DISTANCE
    vd = 2 * HEAD_DIM
    smem = pl.BlockSpec(memory_space=pltpu.SMEM)
    q_spec = [pl.BlockSpec((1, tq, HEAD_DIM), functools.partial(lambda m, b, h, i: (b, i, m * nheads + h), m))
              for m in range(2)]
    k_spec = [pl.BlockSpec((1, s_len, HEAD_DIM),
                           functools.partial(lambda m, b, h, i: (b, 0, (2 + m) * nheads + h), m))
              for m in range(2)]
    v_spec = pl.BlockSpec((1, s_len, vd), lambda b, h, i: (b, 0, 2 * nheads + h))
    return pl.pallas_call(
        functools.partial(_diff_attn_kernel, tq=tq, nheads=nheads, lambda_init=lambda_init),
        grid=(bsz, nheads, s_len // tq),
        in_specs=[smem, pl.BlockSpec((4, HEAD_DIM), lambda b, h, i: (0, 0)),
                  pl.BlockSpec((1, vd), lambda b, h, i: (0, 0)),
                  q_spec[0], q_spec[1], k_spec[0], k_spec[1], v_spec],
        out_specs=pl.BlockSpec((1, tq, vd), lambda b, h, i: (b, i, h)),
        out_shape=jax.ShapeDtypeStruct((bsz, s_len, nheads * vd), BF16),
        scratch_shapes=[pltpu.VMEM((2, 2, tq, tq), F32), pltpu.VMEM((2, tq, LANES), F32),
                        pltpu.VMEM((2, tq, LANES), F32), pltpu.VMEM((2, tq, vd), F32)],
        compiler_params=_params("arbitrary", "arbitrary", "arbitrary"),
        name="diff_attention",
    )(rel_bias, lam_vecs, subln.reshape(1, vd), proj3, proj3, proj3, proj3, proj3)


def _norm_router_kernel(x_ref, g_ref, wr_ref, xn_ref, comb_ref, sel_ref, *, eps, n_experts):
    x = x_ref[...]
    y = x * lax.rsqrt(jnp.mean(x * x, axis=-1, keepdims=True) + eps) * g_ref[...]
    xn_ref[...] = y
    logits = jnp.dot(y, wr_ref[...], preferred_element_type=F32, precision=lax.Precision.HIGHEST)
    lane = lax.broadcasted_iota(I32, logits.shape, 1)
    z = jnp.where(lane < n_experts, logits, NEG_INF)
    v1 = jnp.max(z, axis=-1, keepdims=True)
    i1 = jnp.min(jnp.where(z == v1, lane, LANES), axis=-1, keepdims=True)
    z2 = jnp.where(lane == i1, NEG_INF, z)
    v2 = jnp.max(z2, axis=-1, keepdims=True)
    i2 = jnp.min(jnp.where(z2 == v2, lane, LANES), axis=-1, keepdims=True)
    e2 = jnp.exp(v2 - v1)
    den = 1.0 + e2
    comb_ref[...] = jnp.where(lane == i1, 1.0 / den, 0.0) + jnp.where(lane == i2, e2 / den, 0.0)
    sel_ref[...] = jnp.where(lane == i1, 1.0, 0.0) + jnp.where(lane == i2, 1.0, 0.0)


def norm_router(x2d, g, w_router, eps):
    m, d = x2d.shape
    n_experts = w_router.shape[1]
    tm = _pick_tile(m, 256, SUBLANES)
    wr = jnp.pad(w_router.astype(F32), ((0, 0), (0, LANES - n_experts)))
    lane_spec = pl.BlockSpec((tm, LANES), lambda i: (i, 0))
    return pl.pallas_call(
        functools.partial(_norm_router_kernel, eps=eps, n_experts=n_experts),
        grid=(m // tm,),
        in_specs=[pl.BlockSpec((tm, d), lambda i: (i, 0)), pl.BlockSpec((1, d), lambda i: (0, 0)),
                  pl.BlockSpec((d, LANES), lambda i: (0, 0))],
        out_specs=[pl.BlockSpec((tm, d), lambda i: (i, 0)), lane_spec, lane_spec],
        out_shape=[jax.ShapeDtypeStruct((m, d), F32), jax.ShapeDtypeStruct((m, LANES), F32),
                   jax.ShapeDtypeStruct((m, LANES), F32)],
        compiler_params=_params("parallel"),
        name="norm_router",
    )(x2d, g.reshape(1, d), wr)


MOE_ROW_TILE = 512
PLAN_ROWS = 256


def _moe_plan_kernel(sel_ref, comb_ref, dest_ref, gate_ref, tile_ref, *, n_experts, n_tok):
    rb = PLAN_ROWS
    nblk = n_tok // rb
    shift = MOE_ROW_TILE.bit_length() - 1
    lane = lax.broadcasted_iota(I32, (1, LANES), 1)

    def count_block(b, part):
        m = sel_ref[pl.ds(pl.multiple_of(b * rb, rb), rb), :]
        return part + m.reshape(rb // SUBLANES, SUBLANES, LANES).sum(axis=0)

    cnt = lax.fori_loop(0, nblk, count_block, jnp.zeros((SUBLANES, LANES), F32)).sum(axis=0, keepdims=True)
    padded = (((cnt.astype(I32) + (MOE_ROW_TILE - 1)) >> shift) << shift)
    start = jnp.zeros((1, LANES), I32)
    tile_expert = jnp.zeros((1, LANES), I32)
    for e in range(n_experts):
        begin = jnp.sum(jnp.where(lane < e, padded, 0), axis=-1, keepdims=True)
        end = jnp.sum(jnp.where(lane <= e, padded, 0), axis=-1, keepdims=True)
        start = start + jnp.where(lane == e, begin, 0)
        tile_expert = tile_expert + jnp.where(lane * MOE_ROW_TILE >= end, 1, 0)
    total = jnp.sum(jnp.where(lane < n_experts, padded, 0), axis=-1, keepdims=True)
    tile_ref[...] = jnp.zeros_like(tile_ref)
    tile_ref[0:1, :] = jnp.minimum(tile_expert, n_experts - 1)
    tile_ref[1:2, :] = jnp.where(lane * MOE_ROW_TILE < total, 1, 0)

    start_f = start.astype(F32)
    strict_lower = (lax.broadcasted_iota(I32, (rb, rb), 0) > lax.broadcasted_iota(I32, (rb, rb), 1)).astype(BF16)
    lane_b = lax.broadcasted_iota(I32, (rb, LANES), 1)
    big = float(2 ** 30)

    def rank_block(b, run):
        r0 = pl.multiple_of(b * rb, rb)
        m = sel_ref[pl.ds(r0, rb), :]
        before = jnp.dot(strict_lower, m.astype(BF16), preferred_element_type=F32) + run
        row = start_f + before
        lo = jnp.min(jnp.where(m > 0.0, row, big), axis=-1, keepdims=True)
        hi = jnp.max(jnp.where(m > 0.0, row, -1.0), axis=-1, keepdims=True)
        comb = comb_ref[pl.ds(r0, rb), :]
        g_lo = jnp.sum(jnp.where((m > 0.0) & (row == lo), comb, 0.0), axis=-1, keepdims=True)
        g_hi = jnp.sum(jnp.where((m > 0.0) & (row == hi), comb, 0.0), axis=-1, keepdims=True)
        gate_ref[pl.ds(r0, rb), :] = jnp.where(lane_b == 0, g_lo, jnp.where(lane_b == 1, g_hi, 0.0))
        rows_t = jnp.where(lane_b == 0, lo, jnp.where(lane_b == 1, hi, 0.0)).T
        dest_ref[:, pl.ds(r0, rb)] = rows_t[0:SUBLANES, :].astype(I32)
        return before[rb - 1:rb, :] + m[rb - 1:rb, :]

    lax.fori_loop(0, nblk, rank_block, jnp.zeros((1, LANES), F32))


def moe_plan(sel, comb, n_experts):
    n_tok = sel.shape[0]
    assert n_tok % PLAN_ROWS == 0
    return pl.pallas_call(
        functools.partial(_moe_plan_kernel, n_experts=n_experts, n_tok=n_tok),
        out_shape=[jax.ShapeDtypeStruct((SUBLANES, n_tok), I32), jax.ShapeDtypeStruct((n_tok, LANES), F32),
                   jax.ShapeDtypeStruct((SUBLANES, LANES), I32)],
        compiler_params=pltpu.CompilerParams(vmem_limit_bytes=VMEM_LIMIT_BYTES),
        name="moe_plan",
    )(sel, comb)


def _row_copy(src_ref, src_row, dst_ref, dst_row, sem):
    return pltpu.make_async_copy(src_ref.at[pl.ds(src_row, 1), :], dst_ref.at[pl.ds(dst_row, 1), :], sem)


def _moe_scatter_kernel(dest_ref, x_ref, init_ref, xs_ref, sem, *, tb):
    del init_ref

    def start(n, carry):
        for k in range(TOP_K):
            _row_copy(x_ref, n, xs_ref, dest_ref[k, n], sem).start()
        return carry

    def wait(n, carry):
        for k in range(TOP_K):
            _row_copy(x_ref, n, xs_ref, dest_ref[k, n], sem).wait()
        return carry

    lax.fori_loop(0, tb, start, 0)
    lax.fori_loop(0, tb, wait, 0)


def moe_scatter(dest, xn, rows):
    n_tok, d = xn.shape
    tb = _pick_tile(n_tok, 256)
    any_spec = pl.BlockSpec(memory_space=pl.ANY)
    return pl.pallas_call(
        functools.partial(_moe_scatter_kernel, tb=tb),
        grid=(n_tok // tb,),
        in_specs=[pl.BlockSpec((SUBLANES, tb), lambda i: (0, i), memory_space=pltpu.SMEM),
                  pl.BlockSpec((tb, d), lambda i: (i, 0)), any_spec],
        out_specs=any_spec,
        out_shape=jax.ShapeDtypeStruct((rows, d), xn.dtype),
        scratch_shapes=[pltpu.SemaphoreType.DMA(())],
        input_output_aliases={2: 0},
        compiler_params=_params("arbitrary"),
        name="moe_scatter",
    )(dest, xn, jnp.zeros((rows, d), xn.dtype))


def _moe_up_kernel(te_ref, tv_ref, a_ref, wg_ref, wu_ref, o_ref, abf_ref):
    i, j = pl.program_id(0), pl.program_id(1)

    @pl.when(j == 0)
    def _():
        abf_ref[...] = a_ref[...].astype(BF16)

    @pl.when(tv_ref[i] == 0)
    def _():
        o_ref[...] = jnp.zeros_like(o_ref)

    @pl.when(tv_ref[i] != 0)
    def _():
        a = abf_ref[...]
        g = jnp.dot(a, wg_ref[...], preferred_element_type=F32)
        u = jnp.dot(a, wu_ref[...], preferred_element_type=F32)
        o_ref[...] = ((g * (1.0 / (1.0 + jnp.exp(-g)))) * u).astype(o_ref.dtype)


def moe_up(tile_expert, tile_valid, xs, wg, wu, tn=512):
    rows, kdim = xs.shape
    f = wg.shape[-1]
    tm, tn = MOE_ROW_TILE, _pick_tile(f, tn)
    w_spec = pl.BlockSpec((None, kdim, tn), lambda i, j, te, tv: (te[i], 0, j))
    return pl.pallas_call(
        _moe_up_kernel,
        grid_spec=pltpu.PrefetchScalarGridSpec(
            num_scalar_prefetch=2,
            grid=(rows // tm, f // tn),
            in_specs=[pl.BlockSpec((tm, kdim), lambda i, j, te, tv: (i, 0)), w_spec, w_spec],
            out_specs=pl.BlockSpec((tm, tn), lambda i, j, te, tv: (i, j)),
            scratch_shapes=[pltpu.VMEM((tm, kdim), BF16)]),
        out_shape=jax.ShapeDtypeStruct((rows, f), BF16),
        compiler_params=_params("arbitrary", "arbitrary"),
        name="moe_up",
    )(tile_expert, tile_valid, xs, wg, wu)


def _moe_down_kernel(te_ref, tv_ref, a_ref, w_ref, o_ref):
    i = pl.program_id(0)

    @pl.when(tv_ref[i] == 0)
    def _():
        o_ref[...] = jnp.zeros_like(o_ref)

    @pl.when(tv_ref[i] != 0)
    def _():
        o_ref[...] = jnp.dot(a_ref[...], w_ref[...], preferred_element_type=F32)


def moe_down(tile_expert, tile_valid, up, wd, tn=512):
    rows, f = up.shape
    d = wd.shape[-1]
    tm, tn = MOE_ROW_TILE, _pick_tile(d, tn)
    return pl.pallas_call(
        _moe_down_kernel,
        grid_spec=pltpu.PrefetchScalarGridSpec(
            num_scalar_prefetch=2,
            grid=(rows // tm, d // tn),
            in_specs=[pl.BlockSpec((tm, f), lambda i, j, te, tv: (i, 0)),
                      pl.BlockSpec((None, f, tn), lambda i, j, te, tv: (te[i], 0, j))],
            out_specs=pl.BlockSpec((tm, tn), lambda i, j, te, tv: (i, j))),
        out_shape=jax.ShapeDtypeStruct((rows, d), F32),
        compiler_params=_params("parallel", "parallel"),
        name="moe_down",
    )(tile_expert, tile_valid, up, wd)


def _moe_combine_kernel(dest_ref, gate_ref, h_ref, y_ref, o_ref, ylo_ref, yhi_ref, sem, *, tb):
    bufs = (ylo_ref, yhi_ref)

    def start(n, carry):
        for k in range(TOP_K):
            _row_copy(y_ref, dest_ref[k, n], bufs[k], n, sem).start()
        return carry

    def wait(n, carry):
        for k in range(TOP_K):
            _row_copy(y_ref, dest_ref[k, n], bufs[k], n, sem).wait()
        return carry

    lax.fori_loop(0, tb, start, 0)
    lax.fori_loop(0, tb, wait, 0)
    g = gate_ref[...]
    o_ref[...] = h_ref[...] + (g[:, 0:1] * ylo_ref[...] + g[:, 1:2] * yhi_ref[...])


def moe_combine(dest, gates, h, y):
    n_tok, d = h.shape
    tb = _pick_tile(n_tok, 256)
    return pl.pallas_call(
        functools.partial(_moe_combine_kernel, tb=tb),
        grid=(n_tok // tb,),
        in_specs=[pl.BlockSpec((SUBLANES, tb), lambda i: (0, i), memory_space=pltpu.SMEM),
                  pl.BlockSpec((tb, LANES), lambda i: (i, 0)),
                  pl.BlockSpec((tb, d), lambda i: (i, 0)),
                  pl.BlockSpec(memory_space=pl.ANY)],
        out_specs=pl.BlockSpec((tb, d), lambda i: (i, 0)),
        out_shape=jax.ShapeDtypeStruct((n_tok, d), F32),
        scratch_shapes=[pltpu.VMEM((tb, d), F32), pltpu.VMEM((tb, d), F32), pltpu.SemaphoreType.DMA(())],
        compiler_params=_params("arbitrary"),
        name="moe_combine",
    )(dest, gates, h, y)


def _dsa_swa_layer(h, hn, w_in, kv_norm, w_uk, w_uv, sinks, w_out, rel_bias, bsz, s_len):
    n_tok, d = h.shape
    a_heads = w_uk.shape[0]
    b_heads = sinks.shape[0]
    qa_w, qb_w = a_heads * HEAD_DIM, b_heads * HEAD_DIM
    kv_w = B_KV_HEADS * HEAD_DIM
    qi_w = IDX_HEADS * IDX_DIM
    widths = (qa_w, A_LATENT, qi_w, IDX_DIM, IDX_HEADS, qb_w, kv_w, kv_w)
    off = np.concatenate([[0], np.cumsum(widths)])
    seg = lambda s: w_in[:, int(off[s]):int(off[s + 1])]
    tail_pad = LANES - IDX_DIM - IDX_HEADS
    w_perm = jnp.concatenate([seg(0), seg(5), seg(6), seg(7), seg(1), seg(2), seg(3), seg(4),
                              jnp.zeros((d, tail_pad), w_in.dtype)], axis=1).astype(BF16)
    assert qa_w == qb_w and (qa_w + qb_w) % (2 * kv_w) == 0
    assert (qa_w + qb_w + 2 * kv_w) % A_LATENT == 0 and (qa_w + qb_w + 2 * kv_w + A_LATENT) % qi_w == 0
    kvb_block = (qa_w + qb_w) // (2 * kv_w)
    ckv_block = (qa_w + qb_w + 2 * kv_w) // A_LATENT
    qi_block = (qa_w + qb_w + 2 * kv_w + A_LATENT) // qi_w
    kw_block = (qa_w + qb_w + 2 * kv_w + A_LATENT + qi_w) // LANES

    proj = matmul(hn, w_perm, out_dtype=F32, tn=896)
    proj3 = proj.reshape(bsz, s_len, -1)
    topk = min(TOPK_MAX, s_len // 4)

    packed = latent_pack(proj, ckv_block, kv_norm, RMS_EPS)
    planes = packed.shape[0]
    ckv = jnp.transpose(packed, (1, 0, 2)).reshape(bsz, planes * s_len, LANES)
    idx = dsa_index(proj3, qi_block, kw_block, topk, row_scale=planes)
    q_lat = head_matmul(proj, w_uk.astype(BF16), out_dtype=BF16)
    bias_t = jnp.pad(rel_bias[:, :a_heads].T.astype(F32), ((0, 0), (0, LANES - NUM_BUCKETS)))
    o_lat = dsa_attention(idx, q_lat.reshape(bsz, s_len, a_heads, A_LATENT), ckv, bias_t)
    o_a = head_matmul(o_lat.reshape(n_tok, a_heads * A_LATENT), w_uv.astype(BF16), out_dtype=BF16)

    o_b = swa_attention(proj3, 1, kvb_block, rel_bias.astype(F32), sinks.astype(F32), a_heads)
    o = jnp.concatenate([o_a, o_b.reshape(n_tok, qb_w)], axis=1)
    return matmul(o, w_out.astype(BF16), out_dtype=F32, res=h)


def _diff_layer(h, hn, w_in, lam_vecs, subln, w_out, rel_bias, lambda_init, bsz, s_len):
    n_tok = h.shape[0]
    nheads = w_out.shape[0] // subln.shape[0]
    proj = matmul(hn, w_in.astype(BF16), out_dtype=BF16)
    o = diff_attention(proj.reshape(bsz, s_len, -1), rel_bias.astype(F32), lam_vecs.astype(F32),
                       subln.astype(F32), nheads, lambda_init)
    return matmul(o.reshape(n_tok, -1), w_out.astype(BF16), out_dtype=F32, res=h)


def _moe_layer(h, g, w_router, w_gate, w_up, w_down):
    n_tok, n_experts = h.shape[0], w_router.shape[1]
    xn, comb, sel = norm_router(h, g, w_router, RMS_EPS)
    dest, gates, tiles = moe_plan(sel, comb, n_experts)
    assert (TOP_K * n_tok) % MOE_ROW_TILE == 0
    rows = TOP_K * n_tok + n_experts * MOE_ROW_TILE
    assert rows // MOE_ROW_TILE <= LANES
    xs = moe_scatter(dest, xn, rows)
    up = moe_up(tiles[0], tiles[1], xs, w_gate.astype(BF16), w_up.astype(BF16))
    y = moe_down(tiles[0], tiles[1], up, w_down.astype(BF16))
    return moe_combine(dest, gates, h, y)


def kernel(x, ln_mix, ln_ffn, rel_bias, w_in_ab, kv_norm, w_uk, w_uv, sinks, w_out_ab, w_gate_d, w_up_d,
           w_down_d, w_in_c, lambda_q1, lambda_k1, lambda_q2, lambda_k2, subln, w_out_c, w_router,
           w_gate_e, w_up_e, w_down_e, ln_final):
    bsz, s_len, d = x.shape
    h = x.reshape(bsz * s_len, d)
    for i in range(ln_mix.shape[0]):
        j = i // 2
        hn = rmsnorm(h, ln_mix[i], RMS_EPS, BF16)
        if i % 2 == 0:
            h = _dsa_swa_layer(h, hn, w_in_ab[j], kv_norm[j], w_uk[j], w_uv[j], sinks[j], w_out_ab[j],
                               rel_bias, bsz, s_len)
            hf = rmsnorm(h, ln_ffn[i], RMS_EPS, BF16)
            up = swiglu_up(hf, w_gate_d[j].astype(BF16), w_up_d[j].astype(BF16))
            h = matmul(up, w_down_d[j].astype(BF16), out_dtype=F32, res=h)
        else:
            lambda_init = 0.8 - 0.6 * math.exp(-0.3 * i)
            lam_vecs = jnp.stack([lambda_q1[j], lambda_k1[j], lambda_q2[j], lambda_k2[j]])
            h = _diff_layer(h, hn, w_in_c[j], lam_vecs, subln[j], w_out_c[j], rel_bias, lambda_init,
                            bsz, s_len)
            h = _moe_layer(h, ln_ffn[i], w_router[j], w_gate_e[j], w_up_e[j], w_down_e[j])
    return rmsnorm(h, ln_final, RMS_EPS, x.dtype).reshape(bsz, s_len, d)
```

```python
import functools
import math

import numpy as np
import jax
import jax.numpy as jnp
from jax import lax
from jax.experimental import pallas as pl
from jax.experimental.pallas import tpu as pltpu

F32 = jnp.float32
BF16 = jnp.bfloat16
I32 = jnp.int32

HEAD_DIM = 128
A_LATENT = 512
IDX_HEADS = 16
IDX_DIM = 64
TOPK_MAX = 256
B_KV_HEADS = 2
WINDOW = 128
NUM_BUCKETS = 32
MAX_DISTANCE = 128
TOP_K = 2
RMS_EPS = 1e-6
SUBLN_EPS = 1e-5

LANES = 128
SUBLANES = 8
VMEM_LIMIT_BYTES = 56 * 1024 * 1024
INT_MIN = -(2 ** 31)
NEG_INF = float("-inf")

GATHER_STRIDE = TOPK_MAX + SUBLANES
LATENT_PLANES = A_LATENT // (2 * LANES)
RANKS_PER_STEP = 4
CHUNK_GROUP = 4


def _bucket_thresholds():
    n = np.arange(0, 4 * MAX_DISTANCE, dtype=np.int64)
    max_exact = NUM_BUCKETS // 2
    nf = np.maximum(n, 1).astype(np.float32)
    large = max_exact + (
        np.log(nf / np.float32(max_exact)) / np.float32(math.log(MAX_DISTANCE / max_exact))
        * np.float32(NUM_BUCKETS - max_exact)
    ).astype(np.int32)
    large = np.minimum(large, NUM_BUCKETS - 1)
    bucket = np.where(n < max_exact, n, large)
    assert np.all(np.diff(bucket) >= 0) and bucket[-1] == NUM_BUCKETS - 1
    return tuple(int(np.argmax(bucket >= k)) for k in range(1, NUM_BUCKETS))


BUCKET_THRESHOLDS = _bucket_thresholds()
FAR_DISTANCE = BUCKET_THRESHOLDS[-1]


def _bias_from_distance(dist, table_entry):
    val = jnp.where(dist >= BUCKET_THRESHOLDS[0], table_entry(1), table_entry(0))
    for k in range(2, NUM_BUCKETS):
        val = jnp.where(dist >= BUCKET_THRESHOLDS[k - 1], table_entry(k), val)
    return val


def _pick_tile(dim, target, align=LANES):
    if dim <= target:
        return dim
    t = (target // align) * align
    while t >= align:
        if dim % t == 0:
            return t
        t -= align
    return dim


def _params(*semantics):
    return pltpu.CompilerParams(dimension_semantics=semantics, vmem_limit_bytes=VMEM_LIMIT_BYTES)


def _rmsnorm_kernel(x_ref, g_ref, o_ref, *, eps):
    x = x_ref[...].astype(F32)
    y = x * lax.rsqrt(jnp.mean(x * x, axis=-1, keepdims=True) + eps)
    o_ref[...] = (y * g_ref[...].astype(F32)).astype(o_ref.dtype)


def rmsnorm(x2d, g, eps, out_dtype):
    m, d = x2d.shape
    tm = _pick_tile(m, 256, SUBLANES)
    return pl.pallas_call(
        functools.partial(_rmsnorm_kernel, eps=eps),
        grid=(m // tm,),
        in_specs=[pl.BlockSpec((tm, d), lambda i: (i, 0)), pl.BlockSpec((1, d), lambda i: (0, 0))],
        out_specs=pl.BlockSpec((tm, d), lambda i: (i, 0)),
        out_shape=jax.ShapeDtypeStruct((m, d), out_dtype),
        compiler_params=_params("parallel"),
        name="rmsnorm",
    )(x2d, g.reshape(1, d))


def _matmul_kernel(*refs, nk, has_res, scale_col):
    a_ref, w_ref = refs[0], refs[1]
    pos = 2
    res_ref = scale_ref = None
    if has_res:
        res_ref = refs[pos]
        pos += 1
    if scale_col is not None:
        scale_ref = refs[pos]
        pos += 1
    o_ref = refs[pos]
    k = pl.program_id(2)

    def finish(y):
        if scale_ref is not None:
            y = y * scale_ref[:, scale_col:scale_col + 1]
        if res_ref is not None:
            y = res_ref[...].astype(F32) + y
        o_ref[...] = y.astype(o_ref.dtype)

    part = jnp.dot(a_ref[...], w_ref[...], preferred_element_type=F32)
    if nk == 1:
        finish(part)
        return
    acc_ref = refs[pos + 1]

    @pl.when(k == 0)
    def _():
        acc_ref[...] = part

    @pl.when(k > 0)
    def _():
        acc_ref[...] += part

    @pl.when(k == nk - 1)
    def _():
        finish(acc_ref[...])


def matmul(a, w, *, out_dtype, res=None, row_scale=None, scale_col=None, expert=None,
           tm=1024, tn=512, tk=4096):
    m, kdim = a.shape
    n = w.shape[-1]
    tm, tn, tk = _pick_tile(m, tm, SUBLANES), _pick_tile(n, tn), _pick_tile(kdim, tk)
    nk = kdim // tk
    if expert is None:
        w_spec = pl.BlockSpec((tk, tn), lambda i, j, k: (k, j))
    else:
        w_spec = pl.BlockSpec((None, tk, tn), lambda i, j, k: (expert, k, j))
    in_specs = [pl.BlockSpec((tm, tk), lambda i, j, k: (i, k)), w_spec]
    args = [a, w]
    if res is not None:
        in_specs.append(pl.BlockSpec((tm, tn), lambda i, j, k: (i, j)))
        args.append(res)
    if row_scale is not None:
        in_specs.append(pl.BlockSpec((tm, row_scale.shape[1]), lambda i, j, k: (i, 0)))
        args.append(row_scale)
    return pl.pallas_call(
        functools.partial(_matmul_kernel, nk=nk, has_res=res is not None,
                          scale_col=scale_col if row_scale is not None else None),
        grid=(m // tm, n // tn, nk),
        in_specs=in_specs,
        out_specs=pl.BlockSpec((tm, tn), lambda i, j, k: (i, j)),
        out_shape=jax.ShapeDtypeStruct((m, n), out_dtype),
        scratch_shapes=[pltpu.VMEM((tm, tn), F32)] if nk > 1 else [],
        compiler_params=_params("parallel", "parallel", "arbitrary"),
        name="matmul",
    )(*args)


def _swiglu_up_kernel(a_ref, wg_ref, wu_ref, o_ref, *acc_refs, nk):
    k = pl.program_id(2)

    def finish(g, u):
        silu = g * (1.0 / (1.0 + jnp.exp(-g)))
        o_ref[...] = (silu * u).astype(o_ref.dtype)

    a = a_ref[...]
    gate = jnp.dot(a, wg_ref[...], preferred_element_type=F32)
    up = jnp.dot(a, wu_ref[...], preferred_element_type=F32)
    if nk == 1:
        finish(gate, up)
        return
    accg_ref, accu_ref = acc_refs

    @pl.when(k == 0)
    def _():
        accg_ref[...] = gate
        accu_ref[...] = up

    @pl.when(k > 0)
    def _():
        accg_ref[...] += gate
        accu_ref[...] += up

    @pl.when(k == nk - 1)
    def _():
        finish(accg_ref[...], accu_ref[...])


def swiglu_up(a, wg, wu, *, expert=None, tm=1024, tn=512, tk=4096):
    m, kdim = a.shape
    f = wg.shape[-1]
    tm, tn, tk = _pick_tile(m, tm, SUBLANES), _pick_tile(f, tn), _pick_tile(kdim, tk)
    nk = kdim // tk
    if expert is None:
        w_spec = pl.BlockSpec((tk, tn), lambda i, j, k: (k, j))
    else:
        w_spec = pl.BlockSpec((None, tk, tn), lambda i, j, k: (expert, k, j))
    return pl.pallas_call(
        functools.partial(_swiglu_up_kernel, nk=nk),
        grid=(m // tm, f // tn, nk),
        in_specs=[pl.BlockSpec((tm, tk), lambda i, j, k: (i, k)), w_spec, w_spec],
        out_specs=pl.BlockSpec((tm, tn), lambda i, j, k: (i, j)),
        out_shape=jax.ShapeDtypeStruct((m, f), BF16),
        scratch_shapes=[pltpu.VMEM((tm, tn), F32), pltpu.VMEM((tm, tn), F32)] if nk > 1 else [],
        compiler_params=_params("parallel", "parallel", "arbitrary"),
        name="swiglu_up",
    )(a, wg, wu)


def _head_matmul_kernel(a_ref, w_ref, o_ref):
    o_ref[...] = jnp.dot(a_ref[...].astype(BF16), w_ref[...], preferred_element_type=F32).astype(o_ref.dtype)


def head_matmul(a, w3, *, out_dtype, tm=1024):
    m = a.shape[0]
    nheads, ka, nb = w3.shape
    tm = _pick_tile(m, tm, SUBLANES)
    return pl.pallas_call(
        _head_matmul_kernel,
        grid=(m // tm, nheads),
        in_specs=[pl.BlockSpec((tm, ka), lambda i, h: (i, h)),
                  pl.BlockSpec((None, ka, nb), lambda i, h: (h, 0, 0))],
        out_specs=pl.BlockSpec((tm, nb), lambda i, h: (i, h)),
        out_shape=jax.ShapeDtypeStruct((m, nheads * nb), out_dtype),
        compiler_params=_params("parallel", "parallel"),
        name="head_matmul",
    )(a, w3)


def _bits(x):
    return lax.bitcast_convert_type(x, I32)


def _latent_pack_kernel(c_ref, g_ref, o_ref, *, eps):
    x = c_ref[...]
    y = x * lax.rsqrt(jnp.mean(x * x, axis=-1, keepdims=True) + eps) * g_ref[...]
    yb = _bits(y.astype(BF16).astype(F32))
    for r in range(A_LATENT // (2 * LANES)):
        lo = lax.shift_right_logical(yb[:, (2 * r) * LANES:(2 * r + 1) * LANES], 16)
        hi = yb[:, (2 * r + 1) * LANES:(2 * r + 2) * LANES] & jnp.int32(-65536)
        o_ref[r] = lo | hi


def latent_pack(proj, col_block, g, eps):
    m = proj.shape[0]
    tm = _pick_tile(m, 512, SUBLANES)
    planes = A_LATENT // (2 * LANES)
    return pl.pallas_call(
        functools.partial(_latent_pack_kernel, eps=eps),
        grid=(m // tm,),
        in_specs=[pl.BlockSpec((tm, A_LATENT), lambda i: (i, col_block)),
                  pl.BlockSpec((1, A_LATENT), lambda i: (0, 0))],
        out_specs=pl.BlockSpec((planes, tm, LANES), lambda i: (0, i, 0)),
        out_shape=jax.ShapeDtypeStruct((planes, m, LANES), I32),
        compiler_params=_params("parallel"),
        name="latent_pack",
    )(proj, g.reshape(1, A_LATENT))


def _dsa_index_kernel(qi_ref, kw_ref, kwq_ref, idx_ref, qt_ref, key_ref, rank_ref, acc_ref, run_ref, bnd_ref,
                      *, tq, topk, row_scale):
    ch = tq
    i = pl.program_id(1)
    nchunks = i + 1

    q_t = qi_ref[0].T
    for h in range(IDX_HEADS):
        qt_ref[:, h * tq:(h + 1) * tq] = q_t[h * IDX_DIM:(h + 1) * IDX_DIM, :].astype(BF16)
    w_t = kwq_ref[0].T[IDX_DIM:IDX_DIM + IDX_HEADS, :] * (IDX_DIM ** -0.5 * IDX_HEADS ** -0.5)
    t_row = i * tq + lax.broadcasted_iota(I32, (1, tq), 1)
    s_iota = lax.broadcasted_iota(I32, (ch, tq), 0)

    def chunk_start(c):
        return pl.multiple_of(c * ch, ch)

    group = CHUNK_GROUP
    ngroups = (nchunks + (group - 1)) >> (group.bit_length() - 1)

    def score_group(g, carry):
        rows = group * ch
        g0 = pl.multiple_of(g * rows, rows)
        kc = kw_ref[0, pl.ds(g0, rows), :][:, :IDX_DIM].astype(BF16)
        dots = jnp.dot(kc, qt_ref[...], preferred_element_type=F32)
        for u in range(group):
            sc = jnp.zeros((ch, tq), F32)
            for h in range(IDX_HEADS):
                sc = sc + jnp.maximum(dots[u * ch:(u + 1) * ch, h * tq:(h + 1) * tq], 0.0) * w_t[h:h + 1, :]
            bits = _bits(sc)
            key = bits ^ ((bits >> 31) & jnp.int32(0x7FFFFFFF))
            r0 = g0 + u * ch
            key_ref[pl.ds(r0, ch), :] = jnp.where(r0 + s_iota <= t_row, key, jnp.int32(INT_MIN))
        return carry

    lax.fori_loop(0, ngroups, score_group, 0)

    def column_sum(m):
        return m.reshape(m.shape[0] // SUBLANES, SUBLANES, tq).sum(axis=0)

    def count(preds):
        rows = group * ch

        def body(g, parts):
            k = key_ref[pl.ds(pl.multiple_of(g * rows, rows), rows), :]
            return tuple(part + column_sum(jnp.where(pred(k), 1, 0).astype(I32)) for part, pred in zip(parts, preds))
        parts = lax.fori_loop(0, ngroups, body, tuple(jnp.zeros((SUBLANES, tq), I32) for _ in preds))
        return tuple(part.sum(axis=0, keepdims=True) for part in parts)

    def at_least(cand):
        return lambda k: k >= cand

    thr = jnp.where(count((at_least(0),))[0] >= topk, 0, INT_MIN).astype(I32)

    def next_bit(b, thr):
        cand = thr | lax.shift_left(jnp.int32(1), 30 - b)
        return jnp.where(count((at_least(cand),))[0] >= topk, cand, thr)

    thr = lax.fori_loop(0, 31, next_bit, thr)
    n_gt, n_ge = count((lambda k: k > thr, lambda k: (k >= thr) & (k > INT_MIN)))
    need = (topk - n_gt).astype(F32)
    has_ties = jnp.max(n_ge) > topk

    tri = (lax.broadcasted_iota(I32, (ch, ch), 0) >= lax.broadcasted_iota(I32, (ch, ch), 1)).astype(BF16)

    def select_chunk(with_ties, c, carry):
        run, eq_run = carry
        r0 = chunk_start(c)
        k = key_ref[pl.ds(r0, ch), :]
        if with_ties:
            eq = jnp.where(k == thr, jnp.where(k > INT_MIN, 1.0, 0.0), 0.0)
            eq_cum = jnp.dot(tri, eq.astype(BF16), preferred_element_type=F32) + eq_run
            sel = jnp.where(k > thr, 1.0, jnp.where(eq_cum <= need, eq, 0.0))
            eq_run = eq_cum[ch - 1:ch, :]
        else:
            sel = jnp.where(k >= thr, jnp.where(k > INT_MIN, 1.0, 0.0), 0.0)
        rank_ref[pl.ds(r0, ch), :] = sel * jnp.dot(tri, sel.astype(BF16), preferred_element_type=F32)
        n_sel = column_sum(sel).sum(axis=0, keepdims=True)
        run_ref[c] = run
        bnd_ref[0, c] = jnp.min(run).astype(I32)
        bnd_ref[1, c] = jnp.max(run).astype(I32)
        bnd_ref[2, c] = jnp.max(n_sel).astype(I32)
        return run + n_sel, eq_run

    zero_row = jnp.zeros((1, tq), F32)

    def select_group(with_ties, g, carry):
        for u in range(group):
            carry = select_chunk(with_ties, g * group + u, carry)
        return carry

    @pl.when(has_ties)
    def _():
        lax.fori_loop(0, ngroups, functools.partial(select_group, True), (zero_row, zero_row))

    @pl.when(jnp.logical_not(has_ties))
    def _():
        lax.fori_loop(0, ngroups, functools.partial(select_group, False), (zero_row, zero_row))

    acc_ref[...] = jnp.zeros_like(acc_ref)
    sub_iota = lax.broadcasted_iota(I32, (SUBLANES, tq), 0)
    last_tile = topk // SUBLANES - 1

    def compact_chunk(c, carry):
        r0 = chunk_start(c)
        run_lo, run_hi = bnd_ref[0, c], bnd_ref[1, c]
        run = run_ref[c]
        key_index_1 = (r0 + 1 + s_iota).astype(F32)

        def place_ranks(g, carry):
            rank = rank_ref[pl.ds(r0, ch), :]
            first = g * RANKS_PER_STEP
            slots, keys = [], []
            for u in range(RANKS_PER_STEP):
                r = (first + u).astype(F32)
                hit = jnp.where(rank == r + 1.0, key_index_1, 0.0)
                found = column_sum(hit).sum(axis=0, keepdims=True)
                slots.append(jnp.where(found > 0.0, run + r, -1.0))
                keys.append(found - 1.0)

            def place_tile(t8, carry):
                j0 = pl.multiple_of(t8 * SUBLANES, SUBLANES)
                rows = (j0 + sub_iota).astype(F32)
                cur = acc_ref[pl.ds(j0, SUBLANES), :]
                for slot, key in zip(slots, keys):
                    cur = jnp.where(rows == slot, key, cur)
                acc_ref[pl.ds(j0, SUBLANES), :] = cur
                return carry

            hi_tile = jnp.minimum((run_hi + first + (RANKS_PER_STEP - 1)) >> 3, last_tile)
            lax.fori_loop((run_lo + first) >> 3, hi_tile + 1, place_tile, 0)
            return carry

        steps = (bnd_ref[2, c] + (RANKS_PER_STEP - 1)) >> (RANKS_PER_STEP.bit_length() - 1)
        lax.fori_loop(0, steps, place_ranks, 0)
        return carry

    lax.fori_loop(0, nchunks, compact_chunk, 0)
    idx_ref[0] = (acc_ref[...] * float(row_scale)).T.astype(I32)


def dsa_index(proj3, qi_block, kw_block, topk, row_scale=1, tq=128):
    bsz, s_len, _ = proj3.shape
    tq = min(tq, s_len)
    nq = s_len // tq
    assert nq % CHUNK_GROUP == 0
    return pl.pallas_call(
        functools.partial(_dsa_index_kernel, tq=tq, topk=topk, row_scale=row_scale),
        grid=(bsz, nq),
        in_specs=[pl.BlockSpec((1, tq, IDX_HEADS * IDX_DIM), lambda b, i: (b, i, qi_block)),
                  pl.BlockSpec((1, s_len, LANES), lambda b, i: (b, 0, kw_block)),
                  pl.BlockSpec((1, tq, LANES), lambda b, i: (b, i, kw_block))],
        out_specs=pl.BlockSpec((1, tq, topk), lambda b, i: (b, i, 0)),
        out_shape=jax.ShapeDtypeStruct((bsz, s_len, topk), I32),
        scratch_shapes=[pltpu.VMEM((IDX_DIM, IDX_HEADS * tq), BF16),
                        pltpu.VMEM((s_len, tq), I32),
                        pltpu.VMEM((s_len, tq), F32),
                        pltpu.VMEM((topk, tq), F32),
                        pltpu.VMEM((nq, 1, tq), F32),
                        pltpu.SMEM((3, nq), I32)],
        compiler_params=_params("parallel", "arbitrary"),
        name="dsa_index",
    )(proj3, proj3, proj3)


def _dsa_attn_kernel(idx_s_ref, idx_v_ref, ql_ref, ckv_ref, tab_ref, o_ref, *tile_refs, tq, topk, nheads):
    i = pl.program_id(1)
    planes = LATENT_PLANES
    plane_shift = planes.bit_length() - 1
    assert planes == 1 << plane_shift
    lane_iota = lax.broadcasted_iota(I32, (nheads, topk), 1)
    scale = HEAD_DIM ** -0.5

    def gather(q, tile_ref, lo, hi):
        for mi in range(lo, hi):
            row = pl.multiple_of(idx_s_ref[0, q, mi], planes)
            tile_ref[pl.ds(mi, planes, stride=GATHER_STRIDE), :] = ckv_ref[0, pl.ds(row, planes), :]

    def attend(queries, tiles):
        nq = len(queries)
        xs = [[] for _ in queries]
        for r in range(planes):
            for n in range(nq):
                w = tiles[n][r * GATHER_STRIDE:r * GATHER_STRIDE + topk, :]
                xs[n].append(lax.bitcast_convert_type(lax.shift_left(w, 16), F32).astype(BF16))
                xs[n].append(lax.bitcast_convert_type(w & jnp.int32(-65536), F32).astype(BF16))
            yield
        logits = []
        for n, q in enumerate(queries):
            ql = ql_ref[0, q]
            acc = jnp.zeros((nheads, topk), F32)
            for j, x in enumerate(xs[n]):
                acc = acc + lax.dot_general(ql[:, j * LANES:(j + 1) * LANES], x, (((1,), (1,)), ((), ())),
                                            preferred_element_type=F32)
            logits.append(acc)
        yield
        bias = []
        for q in queries:
            dist = jnp.broadcast_to(i * tq + q - (idx_v_ref[0, q] >> plane_shift), (nheads, topk))
            bias.append(_bias_from_distance(dist, lambda k: tab_ref[:, k:k + 1]))
        yield
        es, inv = [], []
        for n, q in enumerate(queries):
            z = jnp.where(lane_iota < jnp.minimum(i * tq + q + 1, topk), logits[n] * scale + bias[n], NEG_INF)
            e = jnp.exp(z - jnp.max(z, axis=-1, keepdims=True))
            inv.append(1.0 / jnp.sum(e, axis=-1, keepdims=True))
            es.append(e.astype(BF16))
        yield
        for j in range(2 * planes):
            for n, q in enumerate(queries):
                o = jnp.dot(es[n], xs[n][j], preferred_element_type=F32) * inv[n]
                o_ref[0, q, :, j * LANES:(j + 1) * LANES] = o.astype(o_ref.dtype)
            if j % 2 == 1:
                yield

    n_stages = 2 * planes + 3
    rows_per_stage = topk // n_stages

    def attend_while_gathering(queries, tiles, next_queries, next_tiles):
        stages = attend(queries, tiles)
        for s in range(n_stages):
            next(stages)
            hi = topk if s == n_stages - 1 else (s + 1) * rows_per_stage
            for q, tile_ref in zip(next_queries, next_tiles):
                gather(q, tile_ref, s * rows_per_stage, hi)

    half = len(tile_refs) // 2
    first, second = tile_refs[:half], tile_refs[half:]
    for n in range(half):
        gather(n, first[n], 0, topk)

    def query_group(qg, carry):
        q0 = 2 * half * qg
        qa = [q0 + n for n in range(half)]
        qb = [q0 + half + n for n in range(half)]
        qc = [jnp.minimum(q0 + 2 * half + n, tq - 1) for n in range(half)]
        attend_while_gathering(qa, first, qb, second)
        attend_while_gathering(qb, second, qc, first)
        return carry

    lax.fori_loop(0, tq // (2 * half), query_group, 0)


def dsa_attention(idx, q_lat, ckv_packed, bias_t, tq=128):
    bsz, s_len, topk = idx.shape
    nheads = q_lat.shape[2]
    tq = min(tq, s_len)
    planes = LATENT_PLANES
    return pl.pallas_call(
        functools.partial(_dsa_attn_kernel, tq=tq, topk=topk, nheads=nheads),
        grid=(bsz, s_len // tq),
        in_specs=[pl.BlockSpec((1, tq, topk), lambda b, i: (b, i, 0), memory_space=pltpu.SMEM),
                  pl.BlockSpec((1, tq, 1, topk), lambda b, i: (b, i, 0, 0)),
                  pl.BlockSpec((1, tq, nheads, A_LATENT), lambda b, i: (b, i, 0, 0)),
                  pl.BlockSpec((1, planes * s_len, LANES), lambda b, i: (b, 0, 0)),
                  pl.BlockSpec((nheads, LANES), lambda b, i: (0, 0))],
        out_specs=pl.BlockSpec((1, tq, nheads, A_LATENT), lambda b, i: (b, i, 0, 0)),
        out_shape=jax.ShapeDtypeStruct((bsz, s_len, nheads, A_LATENT), BF16),
        scratch_shapes=[pltpu.VMEM((planes * GATHER_STRIDE, LANES), I32)] * 8,
        compiler_params=_params("parallel", "arbitrary"),
        name="dsa_attention",
    )(idx, idx.reshape(bsz, s_len, 1, topk), q_lat, ckv_packed, bias_t)


def _swa_kernel(tab_ref, sink_ref, q_ref, kvp_ref, kvc_ref, o_ref, band_ref, *, nheads, slot0):
    w = WINDOW
    group = nheads // B_KV_HEADS
    b, n = pl.program_id(0), pl.program_id(1)
    rel = (lax.broadcasted_iota(I32, (w, 2 * w), 0) + w) - lax.broadcasted_iota(I32, (w, 2 * w), 1)

    @pl.when((b == 0) & (n == 0))
    def _():
        for hb in range(nheads):
            band_ref[hb] = _bias_from_distance(rel, lambda k: tab_ref[k, slot0 + hb])

    s_global = n * w - w + lax.broadcasted_iota(I32, (w, 2 * w), 1)
    mask = (rel >= 0) & (rel < w) & (s_global >= 0)
    scale = HEAD_DIM ** -0.5
    kv_width = B_KV_HEADS * HEAD_DIM
    for kh in range(B_KV_HEADS):
        ks = slice(kh * HEAD_DIM, (kh + 1) * HEAD_DIM)
        vs = slice(kv_width + kh * HEAD_DIM, kv_width + (kh + 1) * HEAD_DIM)
        kk = jnp.concatenate([kvp_ref[0][:, ks], kvc_ref[0][:, ks]], axis=0).astype(BF16)
        vv = jnp.concatenate([kvp_ref[0][:, vs], kvc_ref[0][:, vs]], axis=0).astype(BF16)
        for g in range(group):
            hb = kh * group + g
            cols = slice(hb * HEAD_DIM, (hb + 1) * HEAD_DIM)
            q = q_ref[0][:, cols].astype(BF16)
            z = lax.dot_general(q, kk, (((1,), (1,)), ((), ())), preferred_element_type=F32) * scale + band_ref[hb]
            z = jnp.where(mask, z, NEG_INF)
            sink = sink_ref[hb]
            m = jnp.maximum(jnp.max(z, axis=-1, keepdims=True), sink)
            e = jnp.exp(z - m)
            den = jnp.sum(e, axis=-1, keepdims=True) + jnp.exp(sink - m)
            o_ref[0, :, cols] = jnp.dot((e / den).astype(BF16), vv, preferred_element_type=F32).astype(o_ref.dtype)


def swa_attention(proj3, q_block, kv_block, rel_bias, sinks, slot0):
    bsz, s_len, _ = proj3.shape
    nheads = sinks.shape[0]
    qw = nheads * HEAD_DIM
    kvw = 2 * B_KV_HEADS * HEAD_DIM
    w = WINDOW
    smem = pl.BlockSpec(memory_space=pltpu.SMEM)
    return pl.pallas_call(
        functools.partial(_swa_kernel, nheads=nheads, slot0=slot0),
        grid=(bsz, s_len // w),
        in_specs=[smem, smem,
                  pl.BlockSpec((1, w, qw), lambda b, n: (b, n, q_block)),
                  pl.BlockSpec((1, w, kvw), lambda b, n: (b, jnp.maximum(n - 1, 0), kv_block)),
                  pl.BlockSpec((1, w, kvw), lambda b, n: (b, n, kv_block))],
        out_specs=pl.BlockSpec((1, w, qw), lambda b, n: (b, n, 0)),
        out_shape=jax.ShapeDtypeStruct((bsz, s_len, qw), BF16),
        scratch_shapes=[pltpu.VMEM((nheads, w, 2 * w), F32)],
        compiler_params=_params("arbitrary", "arbitrary"),
        name="swa_attention",
    )(rel_bias, sinks, proj3, proj3, proj3)


def _diff_attn_kernel(tab_ref, lam_ref, subln_ref, q1_ref, q2_ref, k1_ref, k2_ref, v_ref, o_ref,
                      band_ref, m_ref, l_ref, acc_ref, *, tq, nheads, lambda_init):
    h, qi = pl.program_id(1), pl.program_id(2)
    ii = lax.broadcasted_iota(I32, (tq, tq), 0)
    jj = lax.broadcasted_iota(I32, (tq, tq), 1)
    log2e = 1.4426950408889634
    qk_scale = HEAD_DIM ** -0.5 * log2e
    last = NUM_BUCKETS - 1
    ntile = tq // LANES
    vd = acc_ref.shape[-1]

    @pl.when(qi == 0)
    def _():
        ti = lax.broadcasted_iota(I32, (LANES, LANES), 0)
        tj = lax.broadcasted_iota(I32, (LANES, LANES), 1)
        band_ref[...] = jnp.zeros_like(band_ref)
        for m in range(2):
            slot = m * nheads + h
            far = tab_ref[last, slot]
            for which, offset in ((0, 0), (1, tq)):
                for r in range(ntile):
                    for c in range(ntile):
                        base = (r - c) * LANES + offset
                        if base + (LANES - 1) < 0 or base - (LANES - 1) >= FAR_DISTANCE:
                            continue
                        tile = _bias_from_distance(ti - tj + base, lambda k: tab_ref[k, slot])
                        band_ref[m, which, r * LANES:(r + 1) * LANES, c * LANES:(c + 1) * LANES] = (tile - far) * log2e

    m_ref[...] = jnp.full_like(m_ref, NEG_INF)
    l_ref[...] = jnp.zeros_like(l_ref)
    acc_ref[...] = jnp.zeros_like(acc_ref)
    qs = (q1_ref[0], q2_ref[0])
    k_refs = (k1_ref, k2_ref)

    def block(kj, band_idx, causal):
        r0 = pl.multiple_of(kj * tq, tq)
        v = v_ref[0, pl.ds(r0, tq), :]
        scores = [lax.dot_general(qs[m], k_refs[m][0, pl.ds(r0, tq), :], (((1,), (1,)), ((), ())),
                                  preferred_element_type=F32) for m in range(2)]
        alphas, pvs = [], []
        for m in range(2):
            z = scores[m] * qk_scale
            if band_idx is not None:
                z = z + band_ref[m, band_idx]
            if causal:
                z = jnp.where(jj <= ii, z, NEG_INF)
            zt = [z[:, j * LANES:(j + 1) * LANES] for j in range(ntile)]
            m_old = m_ref[m]
            m_new = jnp.maximum(m_old, jnp.max(functools.reduce(jnp.maximum, zt), axis=-1, keepdims=True))
            alpha = jnp.exp2(m_old - m_new)
            es = [jnp.exp2(t - m_new) for t in zt]
            l_ref[m] = alpha * l_ref[m] + functools.reduce(jnp.add, es)
            m_ref[m] = m_new
            alphas.append(alpha)
            pvs.append(jnp.dot(jnp.concatenate([e.astype(BF16) for e in es], axis=1), v,
                               preferred_element_type=F32))
        for m in range(2):
            for c in range(vd // LANES):
                cols = slice(c * LANES, (c + 1) * LANES)
                acc_ref[m, :, cols] = alphas[m] * acc_ref[m, :, cols] + pvs[m][:, cols]

    def far_block(kj, carry):
        block(kj, None, causal=False)
        return carry

    lax.fori_loop(0, jnp.maximum(qi - 1, 0), far_block, 0)

    @pl.when(qi >= 1)
    def _():
        block(qi - 1, 1, causal=False)

    block(qi, 0, causal=True)

    lam = (jnp.exp(jnp.sum(lam_ref[0:1, :] * lam_ref[1:2, :], axis=-1, keepdims=True))
           - jnp.exp(jnp.sum(lam_ref[2:3, :] * lam_ref[3:4, :], axis=-1, keepdims=True)) + lambda_init)
    outs = [acc_ref[m] / jnp.sum(l_ref[m], axis=-1, keepdims=True) for m in range(2)]
    o = outs[0] - lam * outs[1]
    y = o * lax.rsqrt(jnp.mean(o * o, axis=-1, keepdims=True) + SUBLN_EPS)
    o_ref[0] = ((y * subln_ref[...]) * (1.0 - lambda_init)).astype(o_ref.dtype)


def diff_attention(proj3, rel_bias, lam_vecs, subln, nheads, lambda_init, tq=1024):
    bsz, s_len, _ = proj3.shape
    tq = min(tq, s_len)
    assert tq > FAR_DISTANCE
    vd = 2 * HEAD_DIM
    smem = pl.BlockSpec(memory_space=pltpu.SMEM)
    q_spec = [pl.BlockSpec((1, tq, HEAD_DIM), functools.partial(lambda m, b, h, i: (b, i, m * nheads + h), m))
              for m in range(2)]
    k_spec = [pl.BlockSpec((1, s_len, HEAD_DIM),
                           functools.partial(lambda m, b, h, i: (b, 0, (2 + m) * nheads + h), m))
              for m in range(2)]
    v_spec = pl.BlockSpec((1, s_len, vd), lambda b, h, i: (b, 0, 2 * nheads + h))
    return pl.pallas_call(
        functools.partial(_diff_attn_kernel, tq=tq, nheads=nheads, lambda_init=lambda_init),
        grid=(bsz, nheads, s_len // tq),
        in_specs=[smem, pl.BlockSpec((4, HEAD_DIM), lambda b, h, i: (0, 0)),
                  pl.BlockSpec((1, vd), lambda b, h, i: (0, 0)),
                  q_spec[0], q_spec[1], k_spec[0], k_spec[1], v_spec],
        out_specs=pl.BlockSpec((1, tq, vd), lambda b, h, i: (b, i, h)),
        out_shape=jax.ShapeDtypeStruct((bsz, s_len, nheads * vd), BF16),
        scratch_shapes=[pltpu.VMEM((2, 2, tq, tq), F32), pltpu.VMEM((2, tq, LANES), F32),
                        pltpu.VMEM((2, tq, LANES), F32), pltpu.VMEM((2, tq, vd), F32)],
        compiler_params=_params("arbitrary", "arbitrary", "arbitrary"),
        name="diff_attention",
    )(rel_bias, lam_vecs, subln.reshape(1, vd), proj3, proj3, proj3, proj3, proj3)


def _norm_router_kernel(x_ref, g_ref, wr_ref, xn_ref, comb_ref, sel_ref, *, eps, n_experts):
    x = x_ref[...]
    y = x * lax.rsqrt(jnp.mean(x * x, axis=-1, keepdims=True) + eps) * g_ref[...]
    xn_ref[...] = y
    logits = jnp.dot(y, wr_ref[...], preferred_element_type=F32, precision=lax.Precision.HIGHEST)
    lane = lax.broadcasted_iota(I32, logits.shape, 1)
    z = jnp.where(lane < n_experts, logits, NEG_INF)
    v1 = jnp.max(z, axis=-1, keepdims=True)
    i1 = jnp.min(jnp.where(z == v1, lane, LANES), axis=-1, keepdims=True)
    z2 = jnp.where(lane == i1, NEG_INF, z)
    v2 = jnp.max(z2, axis=-1, keepdims=True)
    i2 = jnp.min(jnp.where(z2 == v2, lane, LANES), axis=-1, keepdims=True)
    e2 = jnp.exp(v2 - v1)
    den = 1.0 + e2
    comb_ref[...] = jnp.where(lane == i1, 1.0 / den, 0.0) + jnp.where(lane == i2, e2 / den, 0.0)
    sel_ref[...] = jnp.where(lane == i1, 1.0, 0.0) + jnp.where(lane == i2, 1.0, 0.0)


def norm_router(x2d, g, w_router, eps):
    m, d = x2d.shape
    n_experts = w_router.shape[1]
    tm = _pick_tile(m, 256, SUBLANES)
    wr = jnp.pad(w_router.astype(F32), ((0, 0), (0, LANES - n_experts)))
    lane_spec = pl.BlockSpec((tm, LANES), lambda i: (i, 0))
    return pl.pallas_call(
        functools.partial(_norm_router_kernel, eps=eps, n_experts=n_experts),
        grid=(m // tm,),
        in_specs=[pl.BlockSpec((tm, d), lambda i: (i, 0)), pl.BlockSpec((1, d), lambda i: (0, 0)),
                  pl.BlockSpec((d, LANES), lambda i: (0, 0))],
        out_specs=[pl.BlockSpec((tm, d), lambda i: (i, 0)), lane_spec, lane_spec],
        out_shape=[jax.ShapeDtypeStruct((m, d), F32), jax.ShapeDtypeStruct((m, LANES), F32),
                   jax.ShapeDtypeStruct((m, LANES), F32)],
        compiler_params=_params("parallel"),
        name="norm_router",
    )(x2d, g.reshape(1, d), wr)


MOE_ROW_TILE = 512
PLAN_ROWS = 256


def _moe_plan_kernel(sel_ref, comb_ref, dest_ref, gate_ref, tile_ref, *, n_experts, n_tok):
    rb = PLAN_ROWS
    nblk = n_tok // rb
    shift = MOE_ROW_TILE.bit_length() - 1
    lane = lax.broadcasted_iota(I32, (1, LANES), 1)

    def count_block(b, part):
        m = sel_ref[pl.ds(pl.multiple_of(b * rb, rb), rb), :]
        return part + m.reshape(rb // SUBLANES, SUBLANES, LANES).sum(axis=0)

    cnt = lax.fori_loop(0, nblk, count_block, jnp.zeros((SUBLANES, LANES), F32)).sum(axis=0, keepdims=True)
    padded = (((cnt.astype(I32) + (MOE_ROW_TILE - 1)) >> shift) << shift)
    start = jnp.zeros((1, LANES), I32)
    tile_expert = jnp.zeros((1, LANES), I32)
    for e in range(n_experts):
        begin = jnp.sum(jnp.where(lane < e, padded, 0), axis=-1, keepdims=True)
        end = jnp.sum(jnp.where(lane <= e, padded, 0), axis=-1, keepdims=True)
        start = start + jnp.where(lane == e, begin, 0)
        tile_expert = tile_expert + jnp.where(lane * MOE_ROW_TILE >= end, 1, 0)
    total = jnp.sum(jnp.where(lane < n_experts, padded, 0), axis=-1, keepdims=True)
    tile_ref[...] = jnp.zeros_like(tile_ref)
    tile_ref[0:1, :] = jnp.minimum(tile_expert, n_experts - 1)
    tile_ref[1:2, :] = jnp.where(lane * MOE_ROW_TILE < total, 1, 0)

    start_f = start.astype(F32)
    strict_lower = (lax.broadcasted_iota(I32, (rb, rb), 0) > lax.broadcasted_iota(I32, (rb, rb), 1)).astype(BF16)
    lane_b = lax.broadcasted_iota(I32, (rb, LANES), 1)
    big = float(2 ** 30)

    def rank_block(b, run):
        r0 = pl.multiple_of(b * rb, rb)
        m = sel_ref[pl.ds(r0, rb), :]
        before = jnp.dot(strict_lower, m.astype(BF16), preferred_element_type=F32) + run
        row = start_f + before
        lo = jnp.min(jnp.where(m > 0.0, row, big), axis=-1, keepdims=True)
        hi = jnp.max(jnp.where(m > 0.0, row, -1.0), axis=-1, keepdims=True)
        comb = comb_ref[pl.ds(r0, rb), :]
        g_lo = jnp.sum(jnp.where((m > 0.0) & (row == lo), comb, 0.0), axis=-1, keepdims=True)
        g_hi = jnp.sum(jnp.where((m > 0.0) & (row == hi), comb, 0.0), axis=-1, keepdims=True)
        gate_ref[pl.ds(r0, rb), :] = jnp.where(lane_b == 0, g_lo, jnp.where(lane_b == 1, g_hi, 0.0))
        rows_t = jnp.where(lane_b == 0, lo, jnp.where(lane_b == 1, hi, 0.0)).T
        dest_ref[:, pl.ds(r0, rb)] = rows_t[0:SUBLANES, :].astype(I32)
        return before[rb - 1:rb, :] + m[rb - 1:rb, :]

    lax.fori_loop(0, nblk, rank_block, jnp.zeros((1, LANES), F32))


def moe_plan(sel, comb, n_experts):
    n_tok = sel.shape[0]
    assert n_tok % PLAN_ROWS == 0
    return pl.pallas_call(
        functools.partial(_moe_plan_kernel, n_experts=n_experts, n_tok=n_tok),
        out_shape=[jax.ShapeDtypeStruct((SUBLANES, n_tok), I32), jax.ShapeDtypeStruct((n_tok, LANES), F32),
                   jax.ShapeDtypeStruct((SUBLANES, LANES), I32)],
        compiler_params=pltpu.CompilerParams(vmem_limit_bytes=VMEM_LIMIT_BYTES),
        name="moe_plan",
    )(sel, comb)


def _row_copy(src_ref, src_row, dst_ref, dst_row, sem):
    return pltpu.make_async_copy(src_ref.at[pl.ds(src_row, 1), :], dst_ref.at[pl.ds(dst_row, 1), :], sem)


def _moe_scatter_kernel(dest_ref, x_ref, init_ref, xs_ref, sem, *, tb):
    del init_ref

    def start(n, carry):
        for k in range(TOP_K):
            _row_copy(x_ref, n, xs_ref, dest_ref[k, n], sem).start()
        return carry

    def wait(n, carry):
        for k in range(TOP_K):
            _row_copy(x_ref, n, xs_ref, dest_ref[k, n], sem).wait()
        return carry

    lax.fori_loop(0, tb, start, 0)
    lax.fori_loop(0, tb, wait, 0)


def moe_scatter(dest, xn, rows):
    n_tok, d = xn.shape
    tb = _pick_tile(n_tok, 256)
    any_spec = pl.BlockSpec(memory_space=pl.ANY)
    return pl.pallas_call(
        functools.partial(_moe_scatter_kernel, tb=tb),
        grid=(n_tok // tb,),
        in_specs=[pl.BlockSpec((SUBLANES, tb), lambda i: (0, i), memory_space=pltpu.SMEM),
                  pl.BlockSpec((tb, d), lambda i: (i, 0)), any_spec],
        out_specs=any_spec,
        out_shape=jax.ShapeDtypeStruct((rows, d), xn.dtype),
        scratch_shapes=[pltpu.SemaphoreType.DMA(())],
        input_output_aliases={2: 0},
        compiler_params=_params("arbitrary"),
        name="moe_scatter",
    )(dest, xn, jnp.zeros((rows, d), xn.dtype))


def _moe_up_kernel(te_ref, tv_ref, a_ref, wg_ref, wu_ref, o_ref, abf_ref):
    i, j = pl.program_id(0), pl.program_id(1)

    @pl.when(j == 0)
    def _():
        abf_ref[...] = a_ref[...].astype(BF16)

    @pl.when(tv_ref[i] == 0)
    def _():
        o_ref[...] = jnp.zeros_like(o_ref)

    @pl.when(tv_ref[i] != 0)
    def _():
        a = abf_ref[...]
        g = jnp.dot(a, wg_ref[...], preferred_element_type=F32)
        u = jnp.dot(a, wu_ref[...], preferred_element_type=F32)
        o_ref[...] = ((g * (1.0 / (1.0 + jnp.exp(-g)))) * u).astype(o_ref.dtype)


def moe_up(tile_expert, tile_valid, xs, wg, wu, tn=512):
    rows, kdim = xs.shape
    f = wg.shape[-1]
    tm, tn = MOE_ROW_TILE, _pick_tile(f, tn)
    w_spec = pl.BlockSpec((None, kdim, tn), lambda i, j, te, tv: (te[i], 0, j))
    return pl.pallas_call(
        _moe_up_kernel,
        grid_spec=pltpu.PrefetchScalarGridSpec(
            num_scalar_prefetch=2,
            grid=(rows // tm, f // tn),
            in_specs=[pl.BlockSpec((tm, kdim), lambda i, j, te, tv: (i, 0)), w_spec, w_spec],
            out_specs=pl.BlockSpec((tm, tn), lambda i, j, te, tv: (i, j)),
            scratch_shapes=[pltpu.VMEM((tm, kdim), BF16)]),
        out_shape=jax.ShapeDtypeStruct((rows, f), BF16),
        compiler_params=_params("arbitrary", "arbitrary"),
        name="moe_up",
    )(tile_expert, tile_valid, xs, wg, wu)


def _moe_down_kernel(te_ref, tv_ref, a_ref, w_ref, o_ref):
    i = pl.program_id(0)

    @pl.when(tv_ref[i] == 0)
    def _():
        o_ref[...] = jnp.zeros_like(o_ref)

    @pl.when(tv_ref[i] != 0)
    def _():
        o_ref[...] = jnp.dot(a_ref[...], w_ref[...], preferred_element_type=F32)


def moe_down(tile_expert, tile_valid, up, wd, tn=512):
    rows, f = up.shape
    d = wd.shape[-1]
    tm, tn = MOE_ROW_TILE, _pick_tile(d, tn)
    return pl.pallas_call(
        _moe_down_kernel,
        grid_spec=pltpu.PrefetchScalarGridSpec(
            num_scalar_prefetch=2,
            grid=(rows // tm, d // tn),
            in_specs=[pl.BlockSpec((tm, f), lambda i, j, te, tv: (i, 0)),
                      pl.BlockSpec((None, f, tn), lambda i, j, te, tv: (te[i], 0, j))],
            out_specs=pl.BlockSpec((tm, tn), lambda i, j, te, tv: (i, j))),
        out_shape=jax.ShapeDtypeStruct((rows, d), F32),
        compiler_params=_params("parallel", "parallel"),
        name="moe_down",
    )(tile_expert, tile_valid, up, wd)


def _moe_combine_kernel(dest_ref, gate_ref, h_ref, y_ref, o_ref, ylo_ref, yhi_ref, sem, *, tb):
    bufs = (ylo_ref, yhi_ref)

    def start(n, carry):
        for k in range(TOP_K):
            _row_copy(y_ref, dest_ref[k, n], bufs[k], n, sem).start()
        return carry

    def wait(n, carry):
        for k in range(TOP_K):
            _row_copy(y_ref, dest_ref[k, n], bufs[k], n, sem).wait()
        return carry

    lax.fori_loop(0, tb, start, 0)
    lax.fori_loop(0, tb, wait, 0)
    g = gate_ref[...]
    o_ref[...] = h_ref[...] + (g[:, 0:1] * ylo_ref[...] + g[:, 1:2] * yhi_ref[...])


def moe_combine(dest, gates, h, y):
    n_tok, d = h.shape
    tb = _pick_tile(n_tok, 256)
    return pl.pallas_call(
        functools.partial(_moe_combine_kernel, tb=tb),
        grid=(n_tok // tb,),
        in_specs=[pl.BlockSpec((SUBLANES, tb), lambda i: (0, i), memory_space=pltpu.SMEM),
                  pl.BlockSpec((tb, LANES), lambda i: (i, 0)),
                  pl.BlockSpec((tb, d), lambda i: (i, 0)),
                  pl.BlockSpec(memory_space=pl.ANY)],
        out_specs=pl.BlockSpec((tb, d), lambda i: (i, 0)),
        out_shape=jax.ShapeDtypeStruct((n_tok, d), F32),
        scratch_shapes=[pltpu.VMEM((tb, d), F32), pltpu.VMEM((tb, d), F32), pltpu.SemaphoreType.DMA(())],
        compiler_params=_params("arbitrary"),
        name="moe_combine",
    )(dest, gates, h, y)


def _dsa_swa_layer(h, hn, w_in, kv_norm, w_uk, w_uv, sinks, w_out, rel_bias, bsz, s_len):
    n_tok, d = h.shape
    a_heads = w_uk.shape[0]
    b_heads = sinks.shape[0]
    qa_w, qb_w = a_heads * HEAD_DIM, b_heads * HEAD_DIM
    kv_w = B_KV_HEADS * HEAD_DIM
    qi_w = IDX_HEADS * IDX_DIM
    widths = (qa_w, A_LATENT, qi_w, IDX_DIM, IDX_HEADS, qb_w, kv_w, kv_w)
    off = np.concatenate([[0], np.cumsum(widths)])
    seg = lambda s: w_in[:, int(off[s]):int(off[s + 1])]
    tail_pad = LANES - IDX_DIM - IDX_HEADS
    w_perm = jnp.concatenate([seg(0), seg(5), seg(6), seg(7), seg(1), seg(2), seg(3), seg(4),
                              jnp.zeros((d, tail_pad), w_in.dtype)], axis=1).astype(BF16)
    assert qa_w == qb_w and (qa_w + qb_w) % (2 * kv_w) == 0
    assert (qa_w + qb_w + 2 * kv_w) % A_LATENT == 0 and (qa_w + qb_w + 2 * kv_w + A_LATENT) % qi_w == 0
    kvb_block = (qa_w + qb_w) // (2 * kv_w)
    ckv_block = (qa_w + qb_w + 2 * kv_w) // A_LATENT
    qi_block = (qa_w + qb_w + 2 * kv_w + A_LATENT) // qi_w
    kw_block = (qa_w + qb_w + 2 * kv_w + A_LATENT + qi_w) // LANES

    proj = matmul(hn, w_perm, out_dtype=F32, tn=896)
    proj3 = proj.reshape(bsz, s_len, -1)
    topk = min(TOPK_MAX, s_len // 4)

    packed = latent_pack(proj, ckv_block, kv_norm, RMS_EPS)
    planes = packed.shape[0]
    ckv = jnp.transpose(packed, (1, 0, 2)).reshape(bsz, planes * s_len, LANES)
    idx = dsa_index(proj3, qi_block, kw_block, topk, row_scale=planes)
    q_lat = head_matmul(proj, w_uk.astype(BF16), out_dtype=BF16)
    bias_t = jnp.pad(rel_bias[:, :a_heads].T.astype(F32), ((0, 0), (0, LANES - NUM_BUCKETS)))
    o_lat = dsa_attention(idx, q_lat.reshape(bsz, s_len, a_heads, A_LATENT), ckv, bias_t)
    o_a = head_matmul(o_lat.reshape(n_tok, a_heads * A_LATENT), w_uv.astype(BF16), out_dtype=BF16)

    o_b = swa_attention(proj3, 1, kvb_block, rel_bias.astype(F32), sinks.astype(F32), a_heads)
    o = jnp.concatenate([o_a, o_b.reshape(n_tok, qb_w)], axis=1)
    return matmul(o, w_out.astype(BF16), out_dtype=F32, res=h)


def _diff_layer(h, hn, w_in, lam_vecs, subln, w_out, rel_bias, lambda_init, bsz, s_len):
    n_tok = h.shape[0]
    nheads = w_out.shape[0] // subln.shape[0]
    proj = matmul(hn, w_in.astype(BF16), out_dtype=BF16)
    o = diff_attention(proj.reshape(bsz, s_len, -1), rel_bias.astype(F32), lam_vecs.astype(F32),
                       subln.astype(F32), nheads, lambda_init)
    return matmul(o.reshape(n_tok, -1), w_out.astype(BF16), out_dtype=F32, res=h)


def _moe_layer(h, g, w_router, w_gate, w_up, w_down):
    n_tok, n_experts = h.shape[0], w_router.shape[1]
    xn, comb, sel = norm_router(h, g, w_router, RMS_EPS)
    dest, gates, tiles = moe_plan(sel, comb, n_experts)
    assert (TOP_K * n_tok) % MOE_ROW_TILE == 0
    rows = TOP_K * n_tok + n_experts * MOE_ROW_TILE
    assert rows // MOE_ROW_TILE <= LANES
    xs = moe_scatter(dest, xn, rows)
    up = moe_up(tiles[0], tiles[1], xs, w_gate.astype(BF16), w_up.astype(BF16))
    y = moe_down(tiles[0], tiles[1], up, w_down.astype(BF16))
    return moe_combine(dest, gates, h, y)


def kernel(x, ln_mix, ln_ffn, rel_bias, w_in_ab, kv_norm, w_uk, w_uv, sinks, w_out_ab, w_gate_d, w_up_d,
           w_down_d, w_in_c, lambda_q1, lambda_k1, lambda_q2, lambda_k2, subln, w_out_c, w_router,
           w_gate_e, w_up_e, w_down_e, ln_final):
    bsz, s_len, d = x.shape
    h = x.reshape(bsz * s_len, d)
    for i in range(ln_mix.shape[0]):
        j = i // 2
        hn = rmsnorm(h, ln_mix[i], RMS_EPS, BF16)
        if i % 2 == 0:
            h = _dsa_swa_layer(h, hn, w_in_ab[j], kv_norm[j], w_uk[j], w_uv[j], sinks[j], w_out_ab[j],
                               rel_bias, bsz, s_len)
            hf = rmsnorm(h, ln_ffn[i], RMS_EPS, BF16)
            up = swiglu_up(hf, w_gate_d[j].astype(BF16), w_up_d[j].astype(BF16))
            h = matmul(up, w_down_d[j].astype(BF16), out_dtype=F32, res=h, tn=1024)
        else:
            lambda_init = 0.8 - 0.6 * math.exp(-0.3 * i)
            lam_vecs = jnp.stack([lambda_q1[j], lambda_k1[j], lambda_q2[j], lambda_k2[j]])
            h = _diff_layer(h, hn, w_in_c[j], lam_vecs, subln[j], w_out_c[j], rel_bias, lambda_init,
                            bsz, s_len)
            h = _moe_layer(h, ln_ffn[i], w_router[j], w_gate_e[j], w_up_e[j], w_down_e[j])
    return rmsnorm(h, ln_final, RMS_EPS, x.dtype).reshape(bsz, s_len, d)
```
